```python
import math
import jax, jax.numpy as jnp
from jax import lax
import numpy as np

D_MODEL = 1024
BATCH = 4
SEQ = 4096
DEPTH = 4

N_A = DEPTH // 2
N_B = DEPTH - N_A
TOK_WIDTH = 3 * D_MODEL // 4
MEM_HEADS = 4
MEM_HEAD_DIM = 64
MEM_WIDTH = MEM_HEADS * MEM_HEAD_DIM
MIX_WIDTH = TOK_WIDTH + MEM_WIDTH
N_MEM = 256
S5_GROUP = 16
S5_GROUPS = TOK_WIDTH // S5_GROUP
S5_STATE = 64
DT_MIN, DT_MAX = 1e-3, 1e-1
DIFF_HEAD_DIM = 64
DIFF_HEADS = TOK_WIDTH // (2 * DIFF_HEAD_DIM)
DIFF_V_DIM = 2 * DIFF_HEAD_DIM
QK_WIDTH = DIFF_HEADS * 2 * DIFF_HEAD_DIM
V_WIDTH = DIFF_HEADS * DIFF_V_DIM
ROT_DIM = DIFF_HEAD_DIM // 4
ROPE_THETA = 500000.0
Q_BLOCK = 128
D_FF = 2816
EPS = 1e-6

kernel_name = 'yoco_s5_diffattn_macaron_memory'


def rms_norm(x, g):
    xf = x.astype(jnp.float32)
    y = xf * lax.rsqrt(jnp.mean(xf * xf, axis=-1, keepdims=True) + EPS)
    return (y * g.astype(jnp.float32)).astype(x.dtype)


def swiglu(h, w_in, w_out):
    gu = h @ w_in
    return (jax.nn.silu(gu[..., :D_FF]) * gu[..., D_FF:]) @ w_out


def rope_tables(positions):
    inv = ROPE_THETA ** (-jnp.arange(0, ROT_DIM, 2, dtype=jnp.float32) / ROT_DIM)
    ang = positions.astype(jnp.float32)[..., None] * inv
    return jnp.cos(ang), jnp.sin(ang)


def partial_rope(t, cos, sin):
    half = ROT_DIM // 2
    c = cos[:, :, None, None, :]
    s = sin[:, :, None, None, :]
    t1 = t[..., :half].astype(jnp.float32)
    t2 = t[..., half:ROT_DIM].astype(jnp.float32)
    r = jnp.concatenate([t1 * c - t2 * s, t2 * c + t1 * s], axis=-1).astype(t.dtype)
    return jnp.concatenate([r, t[..., ROT_DIM:]], axis=-1)


def _ssm_combine(e1, e2):
    a1r, a1i, b1r, b1i = e1
    a2r, a2i, b2r, b2i = e2
    ar = a1r * a2r - a1i * a2i
    ai = a1r * a2i + a1i * a2r
    br = a2r * b1r - a2i * b1i + b2r
    bi = a2r * b1i + a2i * b1r + b2i
    return (ar, ai, br, bi)


def s5_mixer(u, a_re, a_im, log_dt, b_re, b_im, c_re, c_im, d, w_glu):
    f32 = jnp.float32
    bsz, seq, _ = u.shape
    ug = u.astype(f32).reshape(bsz, seq, S5_GROUPS, S5_GROUP)
    dt = jnp.exp(log_dt.astype(f32))[:, None]
    lr, li = a_re.astype(f32), a_im.astype(f32)
    mag = jnp.exp(lr * dt)
    abr, abi = mag * jnp.cos(li * dt), mag * jnp.sin(li * dt)
    den = lr * lr + li * li
    nr, ni = abr - 1.0, abi
    fr = (nr * lr + ni * li) / den
    fi = (ni * lr - nr * li) / den
    br, bi = b_re.astype(f32), b_im.astype(f32)
    bbr = fr[..., None] * br - fi[..., None] * bi
    bbi = fr[..., None] * bi + fi[..., None] * br
    bu_r = jnp.einsum('bsgc,gpc->bsgp', ug, bbr)
    bu_i = jnp.einsum('bsgc,gpc->bsgp', ug, bbi)
    a_r = jnp.broadcast_to(abr, (1, seq) + abr.shape)
    a_i = jnp.broadcast_to(abi, (1, seq) + abi.shape)
    _, _, hr, hi = lax.associative_scan(_ssm_combine, (a_r, a_i, bu_r, bu_i), axis=1)
    y = (jnp.einsum('bsgp,gcp->bsgc', hr, c_re.astype(f32))
         - jnp.einsum('bsgp,gcp->bsgc', hi, c_im.astype(f32))
         + d.astype(f32) * ug)
    y = jax.nn.gelu(y.reshape(bsz, seq, TOK_WIDTH))
    y = y * jax.nn.sigmoid(y @ w_glu.astype(f32))
    return y.astype(u.dtype)


def diff_attention(q, k, v, lam):
    bsz, seq = q.shape[0], q.shape[1]
    nblk = seq // Q_BLOCK
    scale = DIFF_HEAD_DIM ** -0.5
    qb = q.reshape(bsz, nblk, Q_BLOCK, DIFF_HEADS, 2, DIFF_HEAD_DIM).transpose(1, 0, 2, 3, 4, 5)
    kidx = jnp.arange(seq)

    def one_block(args):
        qblk, blk = args
        s = jnp.einsum('bqhcd,bkhcd->bhcqk', qblk, k, preferred_element_type=jnp.float32) * scale
        qidx = blk * Q_BLOCK + jnp.arange(Q_BLOCK)
        mask = kidx[None, :] <= qidx[:, None]
        s = jnp.where(mask, s, -jnp.inf)
        p = jax.nn.softmax(s, axis=-1)
        a = p[:, :, 0] - lam * p[:, :, 1]
        return jnp.einsum('bhqk,bkhe->bqhe', a.astype(v.dtype), v)

    o = lax.map(one_block, (qb, jnp.arange(nblk)))
    return o.transpose(1, 0, 2, 3, 4).reshape(bsz, seq, DIFF_HEADS, DIFF_V_DIM)


def memory_attention(q, mk, mv):
    s = jnp.einsum('bshd,bmhd->bhsm', q, mk, preferred_element_type=jnp.float32) * MEM_HEAD_DIM ** -0.5
    p = jax.nn.softmax(s, axis=-1)
    return jnp.einsum('bhsm,bmhd->bshd', p.astype(mv.dtype), mv)


def setup_inputs(seed: int = 0) -> dict:
    key = jax.random.key(seed)
    ks = iter(jax.random.split(key, 48))
    f32 = jnp.float32

    def nrm(shape, scale):
        return jax.random.normal(next(ks), shape, f32) * scale

    def gain(shape):
        return 1.0 + nrm(shape, 0.02)

    x = jax.random.normal(next(ks), (BATCH, SEQ, D_MODEL), f32)
    mem = jax.random.normal(next(ks), (BATCH, N_MEM, D_MODEL), f32)
    offset = jax.random.randint(next(ks), (BATCH, 1), 0, 1024, dtype=jnp.int32)
    positions = offset + jnp.arange(SEQ, dtype=jnp.int32)[None, :]
    n_idx = jnp.arange(S5_STATE, dtype=f32)
    return {
        'x': x,
        'mem': mem,
        'positions': positions,
        'ln_ffn1': gain((DEPTH, D_MODEL)),
        'ffn1_in': nrm((DEPTH, D_MODEL, 2 * D_FF), D_MODEL ** -0.5),
        'ffn1_out': nrm((DEPTH, D_FF, D_MODEL), D_FF ** -0.5),
        'ln_mix': gain((DEPTH, D_MODEL)),
        'w_mix_in': nrm((DEPTH, D_MODEL, MIX_WIDTH), D_MODEL ** -0.5),
        'w_mix_out': nrm((DEPTH, MIX_WIDTH, D_MODEL), MIX_WIDTH ** -0.5),
        'ln_mem': gain((D_MODEL,)),
        'w_mem_kv': nrm((DEPTH, D_MODEL, 2 * MEM_WIDTH), D_MODEL ** -0.5),
        'ln_ffn2': gain((DEPTH, D_MODEL)),
        'ffn2_in': nrm((DEPTH, D_MODEL, 2 * D_FF), D_MODEL ** -0.5),
        'ffn2_out': nrm((DEPTH, D_FF, D_MODEL), D_FF ** -0.5),
        's5_a_re': -0.5 + nrm((N_A, S5_GROUPS, S5_STATE), 0.01),
        's5_a_im': math.pi * n_idx + nrm((N_A, S5_GROUPS, S5_STATE), 0.01),
        's5_log_dt': jax.random.uniform(next(ks), (N_A, S5_GROUPS), f32, math.log(DT_MIN), math.log(DT_MAX)),
        's5_b_re': nrm((N_A, S5_GROUPS, S5_STATE, S5_GROUP), (2 * S5_GROUP) ** -0.5),
        's5_b_im': nrm((N_A, S5_GROUPS, S5_STATE, S5_GROUP), (2 * S5_GROUP) ** -0.5),
        's5_c_re': nrm((N_A, S5_GROUPS, S5_GROUP, S5_STATE), (2 * S5_STATE) ** -0.5),
        's5_c_im': nrm((N_A, S5_GROUPS, S5_GROUP, S5_STATE), (2 * S5_STATE) ** -0.5),
        's5_d': nrm((N_A, S5_GROUPS, S5_GROUP), 1.0),
        's5_w_glu': nrm((N_A, TOK_WIDTH, TOK_WIDTH), TOK_WIDTH ** -0.5),
        'ln_kv': gain((D_MODEL,)),
        'w_kv_shared': nrm((D_MODEL, QK_WIDTH + V_WIDTH), D_MODEL ** -0.5),
        'diff_lq1': nrm((N_B, DIFF_HEAD_DIM), 0.1),
        'diff_lk1': nrm((N_B, DIFF_HEAD_DIM), 0.1),
        'diff_lq2': nrm((N_B, DIFF_HEAD_DIM), 0.1),
        'diff_lk2': nrm((N_B, DIFF_HEAD_DIM), 0.1),
        'diff_subln': gain((N_B, DIFF_V_DIM)),
        'ln_final': gain((D_MODEL,)),
    }


def reference(x, mem, positions, ln_ffn1, ffn1_in, ffn1_out, ln_mix, w_mix_in, w_mix_out,
              ln_mem, w_mem_kv, ln_ffn2, ffn2_in, ffn2_out,
              s5_a_re, s5_a_im, s5_log_dt, s5_b_re, s5_b_im, s5_c_re, s5_c_im, s5_d, s5_w_glu,
              ln_kv, w_kv_shared, diff_lq1, diff_lk1, diff_lq2, diff_lk2, diff_subln, ln_final):
    bsz, seq, _ = x.shape
    n_mem = mem.shape[1]
    cos, sin = rope_tables(positions)
    mem_n = rms_norm(mem, ln_mem)
    k_sh = None
    v_sh = None
    for i in range(DEPTH):
        if i == N_A:
            hk = rms_norm(x, ln_kv)
            kv = hk @ w_kv_shared
            k_sh = partial_rope(kv[..., :QK_WIDTH].reshape(bsz, seq, DIFF_HEADS, 2, DIFF_HEAD_DIM), cos, sin)
            v_sh = kv[..., QK_WIDTH:].reshape(bsz, seq, DIFF_HEADS, DIFF_V_DIM)
        x = x + 0.5 * swiglu(rms_norm(x, ln_ffn1[i]), ffn1_in[i], ffn1_out[i])
        h = rms_norm(x, ln_mix[i])
        proj = h @ w_mix_in[i]
        tok_in = proj[..., :TOK_WIDTH]
        mq = proj[..., TOK_WIDTH:].reshape(bsz, seq, MEM_HEADS, MEM_HEAD_DIM)
        if i < N_A:
            j = i
            tok_out = s5_mixer(tok_in, s5_a_re[j], s5_a_im[j], s5_log_dt[j], s5_b_re[j], s5_b_im[j],
                               s5_c_re[j], s5_c_im[j], s5_d[j], s5_w_glu[j])
        else:
            j = i - N_A
            q = partial_rope(tok_in.reshape(bsz, seq, DIFF_HEADS, 2, DIFF_HEAD_DIM), cos, sin)
            lam_init = 0.8 - 0.6 * math.exp(-0.3 * i)
            lam = (jnp.exp(jnp.sum(diff_lq1[j].astype(jnp.float32) * diff_lk1[j].astype(jnp.float32)))
                   - jnp.exp(jnp.sum(diff_lq2[j].astype(jnp.float32) * diff_lk2[j].astype(jnp.float32)))
                   + lam_init)
            o = diff_attention(q, k_sh, v_sh, lam)
            o = rms_norm(o, diff_subln[j]) * (1.0 - lam_init)
            tok_out = o.reshape(bsz, seq, TOK_WIDTH)
        mkv = mem_n @ w_mem_kv[i]
        mk = mkv[..., :MEM_WIDTH].reshape(bsz, n_mem, MEM_HEADS, MEM_HEAD_DIM)
        mv = mkv[..., MEM_WIDTH:].reshape(bsz, n_mem, MEM_HEADS, MEM_HEAD_DIM)
        mo = memory_attention(mq, mk, mv).reshape(bsz, seq, MEM_WIDTH)
        x = x + jnp.concatenate([tok_out, mo], axis=-1) @ w_mix_out[i]
        x = x + 0.5 * swiglu(rms_norm(x, ln_ffn2[i]), ffn2_in[i], ffn2_out[i])
    return rms_norm(x, ln_final)
```

```python
import functools
import math

import jax
import jax.numpy as jnp
from jax import lax
from jax.experimental import pallas as pl
from jax.experimental.pallas import tpu as pltpu

F32 = jnp.float32
BF16 = jnp.bfloat16

D_MODEL = 1024
DEPTH = 4
N_A = DEPTH // 2
TOK_WIDTH = 768
MEM_HEADS = 4
MEM_HEAD_DIM = 64
MEM_WIDTH = MEM_HEADS * MEM_HEAD_DIM
N_MEM = 256
S5_GROUP = 16
S5_GROUPS = TOK_WIDTH // S5_GROUP
S5_STATE = 64
S5_CHUNK = 16
S5_FLAT = S5_CHUNK * S5_GROUP
DIFF_HEAD_DIM = 64
DIFF_HEADS = 6
DIFF_V_DIM = 128
QK_WIDTH = 768
ROT_DIM = 16
ROPE_THETA = 500000.0
D_FF = 2816
EPS = 1e-6

LANES = 128
VMEM_LIMIT = 56 * 1024 * 1024

_NT = (((1,), (1,)), ((), ()))


def _params(sem, vmem=VMEM_LIMIT):
    return pltpu.CompilerParams(dimension_semantics=sem, vmem_limit_bytes=vmem)


def _resident(shape):
    nd = len(shape)
    return pl.BlockSpec(shape, lambda *_: (0,) * nd, pipeline_mode=pl.Buffered(1))


def _rms(x, g):
    return x * lax.rsqrt(jnp.mean(x * x, axis=-1, keepdims=True) + EPS) * g


FFN_TM = 512
FFN_TF = 256


def _ffn_kernel(x_ref, g_ref, win_ref, wout_ref, *rest, final):
    if final:
        gf_ref, o_ref = rest
    else:
        (o_ref,) = rest
    x = x_ref[...]
    h = _rms(x, g_ref[...]).astype(BF16)
    acc = x
    for c in range(D_FF // FFN_TF):
        lo = c * FFN_TF
        gate = jnp.dot(h, win_ref[:, lo:lo + FFN_TF], preferred_element_type=F32)
        up = jnp.dot(h, win_ref[:, D_FF + lo:D_FF + lo + FFN_TF], preferred_element_type=F32)
        a = (gate * jax.nn.sigmoid(gate) * (0.5 * up)).astype(BF16)
        acc = acc + jnp.dot(a, wout_ref[lo:lo + FFN_TF, :], preferred_element_type=F32)
    if final:
        acc = _rms(acc, gf_ref[...])
    o_ref[...] = acc


def _ffn(x, g, w_in, w_out, final_g=None):
    n, d = x.shape
    row = pl.BlockSpec((FFN_TM, d), lambda i: (i, 0))
    in_specs = [row, _resident((1, d)), _resident(w_in.shape), _resident(w_out.shape)]
    args = [x, g.reshape(1, d), w_in, w_out]
    if final_g is not None:
        in_specs.append(_resident((1, d)))
        args.append(final_g.reshape(1, d))
    return pl.pallas_call(
        functools.partial(_ffn_kernel, final=final_g is not None),
        grid=(n // FFN_TM,),
        in_specs=in_specs,
        out_specs=row,
        out_shape=jax.ShapeDtypeStruct((n, d), F32),
        compiler_params=_params(("parallel",)),
        name="ffn",
    )(*args)


PROJ_TM = 512


def _rope_table_kernel(pos_ref, inv_ref, c_ref, sa_ref, sb_ref):
    ang = pos_ref[...].astype(F32) * inv_ref[...]
    lane = lax.broadcasted_iota(jnp.int32, ang.shape, 1) % DIFF_HEAD_DIM
    c = jnp.cos(ang)
    s = jnp.sin(ang)
    half = ROT_DIM // 2
    c_ref[...] = jnp.where(lane < ROT_DIM, c, 1.0)
    sa_ref[...] = jnp.where(lane < half, -s, 0.0)
    sb_ref[...] = jnp.where((lane >= half) & (lane < ROT_DIM), s, 0.0)


def _rope_tables(positions):
    n = positions.size
    half = ROT_DIM // 2
    inv = ROPE_THETA ** (-jnp.arange(0, ROT_DIM, 2, dtype=F32) / ROT_DIM)
    inv_lane = jnp.tile(inv, LANES // half).reshape(1, LANES)
    tm = 1024
    out = pl.BlockSpec((tm, LANES), lambda i: (i, 0))
    return pl.pallas_call(
        _rope_table_kernel,
        grid=(n // tm,),
        in_specs=[pl.BlockSpec((tm, 1), lambda i: (i, 0)), _resident((1, LANES))],
        out_specs=[out, out, out],
        out_shape=[jax.ShapeDtypeStruct((n, LANES), F32)] * 3,
        compiler_params=_params(("parallel",)),
        name="rope_tables",
    )(positions.reshape(n, 1), inv_lane)


def _norm_proj_kernel(x_ref, g_ref, w_ref, *rest, rope_cols):
    o_ref = rest[-1]
    h = _rms(x_ref[...], g_ref[...]).astype(BF16)
    y = jnp.dot(h, w_ref[...], preferred_element_type=F32)
    if rope_cols:
        c_ref, sa_ref, sb_ref = rest[:3]
        c, sa, sb = c_ref[...], sa_ref[...], sb_ref[...]
        half = ROT_DIM // 2
        for t in range(rope_cols // LANES):
            blk = y[:, t * LANES:(t + 1) * LANES]
            r = (blk * c + pltpu.roll(blk, LANES - half, 1) * sa + pltpu.roll(blk, half, 1) * sb)
            o_ref[:, t * LANES:(t + 1) * LANES] = r.astype(o_ref.dtype)
        if rope_cols < y.shape[1]:
            o_ref[:, rope_cols:] = y[:, rope_cols:].astype(o_ref.dtype)
    else:
        o_ref[...] = y.astype(o_ref.dtype)


def _norm_proj(x, g, w, rope=None, rope_cols=0):
    n, d = x.shape
    nout = w.shape[1]
    tm = min(PROJ_TM, n)
    in_specs = [pl.BlockSpec((tm, d), lambda i: (i, 0)), _resident((1, d)), _resident(w.shape)]
    args = [x, g.reshape(1, d), w]
    if rope_cols:
        tab = pl.BlockSpec((tm, LANES), lambda i: (i, 0))
        in_specs += [tab, tab, tab]
        args += list(rope)
    return pl.pallas_call(
        functools.partial(_norm_proj_kernel, rope_cols=rope_cols),
        grid=(n // tm,),
        in_specs=in_specs,
        out_specs=pl.BlockSpec((tm, nout), lambda i: (i, 0)),
        out_shape=jax.ShapeDtypeStruct((n, nout), BF16),
        compiler_params=_params(("parallel",)),
        name="norm_proj",
    )(*args)


S5_POW_ROWS = 24


def _s5_prep_kernel(are_ref, aim_ref, ldt_ref, btr_ref, bti_ref, cr_ref, ci_ref, dd_ref,
                    t_ref, wend_ref, et_ref, da_ref, db_ref, e_sc, *, n_levels):
    lr = are_ref[...]
    li = aim_ref[...]
    dt = jnp.exp(ldt_ref[...])
    lane = lax.broadcasted_iota(jnp.int32, (1, LANES), 1)
    lo = lane < S5_STATE

    def powers(k):
        mag = jnp.exp(k * (lr * dt))
        ang = k * (li * dt)
        return mag * jnp.cos(ang), mag * jnp.sin(ang)

    kf = lax.broadcasted_iota(jnp.int32, (S5_POW_ROWS, 1), 0).astype(F32)
    pr, pi = powers(kf)
    p_ri = jnp.where(lo, pr, pi)
    p_mir = jnp.where(lo, -pi, pr)
    p_rmi = jnp.where(lo, pr, -pi)
    p_mimr = jnp.where(lo, -pi, -pr)

    abr, abi = pr[1:2], pi[1:2]
    den = lr * lr + li * li
    nr, ni = abr - 1.0, abi
    fr = (nr * lr + ni * li) / den
    fi = (ni * lr - nr * li) / den
    btr, bti = btr_ref[...], bti_ref[...]
    bbr = fr * btr - fi * bti
    bbi = fr * bti + fi * btr

    cr, ci = cr_ref[...], ci_ref[...]
    for k in range(S5_CHUNK + 1):
        e_sc[k * S5_GROUP:(k + 1) * S5_GROUP, :] = cr * p_rmi[k:k + 1] + ci * p_mimr[k:k + 1]
    et_ref[...] = e_sc[S5_GROUP:, :].astype(et_ref.dtype)

    for s in range(S5_CHUNK):
        k = S5_CHUNK - 1 - s
        w = bbr * p_ri[k:k + 1] + bbi * p_mir[k:k + 1]
        wend_ref[s * S5_GROUP:(s + 1) * S5_GROUP, :] = w.astype(wend_ref.dtype)

    bb = jnp.where(lo, bbr, bbi)
    kern = lax.dot_general(bb, e_sc[:S5_FLAT, :], _NT, precision=lax.Precision.HIGHEST,
                           preferred_element_type=F32)
    col = lax.broadcasted_iota(jnp.int32, (S5_GROUP, S5_FLAT), 1)
    row = lax.broadcasted_iota(jnp.int32, (S5_GROUP, S5_FLAT), 0)
    dd = dd_ref[...]
    for s in range(S5_CHUNK):
        shifted = kern if s == 0 else pltpu.roll(kern, s * S5_GROUP, 1)
        blk = jnp.where(col >= s * S5_GROUP, shifted, 0.0)
        blk = blk + jnp.where(col == row + s * S5_GROUP, dd, 0.0)
        t_ref[s * S5_GROUP:(s + 1) * S5_GROUP, :] = blk.astype(t_ref.dtype)

    steps = (S5_CHUNK * 2.0 ** lax.broadcasted_iota(jnp.int32, (n_levels, 1), 0).astype(F32))
    qr, qi = powers(steps)
    da_ref[...] = qr
    db_ref[...] = jnp.where(lo, -qi, qi)


def _dup(a):
    return jnp.concatenate([a, a], axis=-1)


def _s5_prep(a_re, a_im, log_dt, b_re, b_im, c_re, c_im, d, n_levels):
    g = S5_GROUPS
    args = [
        _dup(a_re).reshape(g, 1, LANES), _dup(a_im).reshape(g, 1, LANES), log_dt.reshape(g, 1, 1),
        _dup(jnp.swapaxes(b_re, 1, 2)), _dup(jnp.swapaxes(b_im, 1, 2)), _dup(c_re), _dup(c_im),
        jnp.tile(d, (1, S5_CHUNK)).reshape(g, 1, S5_FLAT),
    ]

    def per_group(*shape):
        return pl.BlockSpec((None,) + shape, lambda i: (i,) + (0,) * len(shape))

    in_specs = [per_group(1, LANES), per_group(1, LANES), per_group(1, 1),
                per_group(S5_GROUP, LANES), per_group(S5_GROUP, LANES),
                per_group(S5_GROUP, LANES), per_group(S5_GROUP, LANES), per_group(1, S5_FLAT)]
    out_specs = [per_group(S5_FLAT, S5_FLAT), per_group(S5_FLAT, LANES), per_group(S5_FLAT, LANES),
                 per_group(n_levels, LANES), per_group(n_levels, LANES)]
    out_shape = [jax.ShapeDtypeStruct((g, S5_FLAT, S5_FLAT), BF16),
                 jax.ShapeDtypeStruct((g, S5_FLAT, LANES), BF16),
                 jax.ShapeDtypeStruct((g, S5_FLAT, LANES), BF16),
                 jax.ShapeDtypeStruct((g, n_levels, LANES), F32),
                 jax.ShapeDtypeStruct((g, n_levels, LANES), F32)]
    return pl.pallas_call(
        functools.partial(_s5_prep_kernel, n_levels=n_levels),
        grid=(g,),
        in_specs=in_specs,
        out_specs=out_specs,
        out_shape=out_shape,
        scratch_shapes=[pltpu.VMEM(((S5_CHUNK + 1) * S5_GROUP, LANES), F32)],
        compiler_params=_params(("parallel",)),
        name="s5_prep",
    )(*args)


def _s5_main_kernel(uf_ref, t_ref, wend_ref, et_ref, da_ref, db_ref, y_ref, *, n_levels):
    nb, r, _ = uf_ref.shape
    u = uf_ref[...].reshape(nb * r, S5_FLAT)
    y = jnp.dot(u, t_ref[...], preferred_element_type=F32)
    hc = jnp.dot(u, wend_ref[...], preferred_element_type=F32)
    da, db = da_ref[...], db_ref[...]
    rows = lax.broadcasted_iota(jnp.int32, (r, LANES), 0)
    hprev = []
    for b in range(nb):
        x = hc[b * r:(b + 1) * r]
        for m in range(n_levels):
            d = 1 << m
            xs = jnp.where(rows >= d, pltpu.roll(x, d, 0), 0.0)
            x = x + xs * da[m:m + 1] + pltpu.roll(xs, S5_STATE, 1) * db[m:m + 1]
        hprev.append(jnp.where(rows >= 1, pltpu.roll(x, 1, 0), 0.0))
    hp = jnp.concatenate(hprev, axis=0).astype(BF16)
    y = y + lax.dot_general(hp, et_ref[...], _NT, preferred_element_type=F32)
    y_ref[...] = y.reshape(nb, r, S5_FLAT).astype(y_ref.dtype)


def _s5_main(uf, ops, n_levels):
    nb, g, r, _ = uf.shape
    t, wend, et, da, db = ops

    def per_group(*shape):
        return pl.BlockSpec((None,) + shape, lambda i: (i,) + (0,) * len(shape))

    io = pl.BlockSpec((nb, None, r, S5_FLAT), lambda i: (0, i, 0, 0))
    return pl.pallas_call(
        functools.partial(_s5_main_kernel, n_levels=n_levels),
        grid=(g,),
        in_specs=[io, per_group(S5_FLAT, S5_FLAT), per_group(S5_FLAT, LANES),
                  per_group(S5_FLAT, LANES), per_group(n_levels, LANES), per_group(n_levels, LANES)],
        out_specs=io,
        out_shape=jax.ShapeDtypeStruct(uf.shape, BF16),
        compiler_params=_params(("parallel",)),
        name="s5_main",
    )(uf, t, wend, et, da, db)


S5_POST_TM = 1024


def _s5_post_kernel(y_ref, w_ref, o_ref):
    z = jax.nn.gelu(y_ref[...].astype(F32), approximate=True)
    gate = jnp.dot(z.astype(BF16), w_ref[...], preferred_element_type=F32)
    o_ref[...] = (z * jax.nn.sigmoid(gate)).astype(o_ref.dtype)


def _s5_post(y, w_glu):
    n, w = y.shape
    tm = min(S5_POST_TM, n)
    row = pl.BlockSpec((tm, w), lambda i: (i, 0))
    return pl.pallas_call(
        _s5_post_kernel,
        grid=(n // tm,),
        in_specs=[row, _resident(w_glu.shape)],
        out_specs=row,
        out_shape=jax.ShapeDtypeStruct((n, w), BF16),
        compiler_params=_params(("parallel",)),
        name="s5_post",
    )(y, w_glu)


def _s5_mixer(u, bsz, seq, prep_args, w_glu):
    r = seq // S5_CHUNK
    n_levels = int(math.log2(r))
    assert 1 << n_levels == r
    ops = _s5_prep(*prep_args, n_levels=n_levels)
    uf = u.reshape(bsz, r, S5_CHUNK, S5_GROUPS, S5_GROUP).transpose(0, 3, 1, 2, 4)
    yf = _s5_main(uf.reshape(bsz, S5_GROUPS, r, S5_FLAT), ops, n_levels)
    y = yf.reshape(bsz, S5_GROUPS, r, S5_CHUNK, S5_GROUP).transpose(0, 2, 3, 1, 4)
    return _s5_post(y.reshape(bsz * seq, TOK_WIDTH), w_glu)


ATTN_T = 256


def _diff_attn_kernel(q_ref, k_ref, v_ref, lq1_ref, lk1_ref, lq2_ref, lk2_ref, g_ref, o_ref,
                      *, lam_init):
    t = ATTN_T
    i = pl.program_id(2)
    q = q_ref[...] * jnp.asarray(DIFF_HEAD_DIM ** -0.5, BF16)
    qs = [q[:, c * DIFF_HEAD_DIM:(c + 1) * DIFF_HEAD_DIM] for c in range(2)]

    def step(kb, vb, carry, mask):
        out = []
        for c in range(2):
            m, l, a = carry[c]
            s = lax.dot_general(qs[c], kb[:, c * DIFF_HEAD_DIM:(c + 1) * DIFF_HEAD_DIM], _NT,
                                preferred_element_type=F32)
            if mask is not None:
                s = jnp.where(mask, s, -jnp.inf)
            m_new = jnp.maximum(m, jnp.max(s, axis=1, keepdims=True))
            alpha = jnp.exp(m - m_new)
            p = jnp.exp(s - m_new)
            l = alpha * l + jnp.sum(p, axis=1, keepdims=True)
            a = alpha * a + jnp.dot(p.astype(BF16), vb, preferred_element_type=F32)
            out.append((m_new, l, a))
        return tuple(out)

    def body(j, carry):
        off = pl.multiple_of(j * t, t)
        return step(k_ref[pl.ds(off, t), :], v_ref[pl.ds(off, t), :], carry, None)

    init = tuple((jnp.full((t, 1), -1e30, F32), jnp.zeros((t, 1), F32),
                  jnp.zeros((t, DIFF_V_DIM), F32)) for _ in range(2))
    carry = lax.fori_loop(0, i, body, init)
    off = pl.multiple_of(i * t, t)
    causal = (lax.broadcasted_iota(jnp.int32, (t, t), 1) <= lax.broadcasted_iota(jnp.int32, (t, t), 0))
    (_, l1, a1), (_, l2, a2) = step(k_ref[pl.ds(off, t), :], v_ref[pl.ds(off, t), :], carry, causal)

    lam = (jnp.exp(jnp.sum(lq1_ref[...] * lk1_ref[...], axis=-1, keepdims=True))
           - jnp.exp(jnp.sum(lq2_ref[...] * lk2_ref[...], axis=-1, keepdims=True)) + lam_init)
    o = a1 / l1 - lam * (a2 / l2)
    o_ref[...] = (_rms(o, g_ref[...]) * (1.0 - lam_init)).astype(o_ref.dtype)


def _diff_attn(proj, kv, bsz, seq, lq1, lk1, lq2, lk2, subln, lam_init):
    n = proj.shape[0]
    t = ATTN_T
    nq = seq // t
    vec = _resident((1, DIFF_HEAD_DIM))
    return pl.pallas_call(
        functools.partial(_diff_attn_kernel, lam_init=lam_init),
        grid=(bsz, DIFF_HEADS, nq),
        in_specs=[
            pl.BlockSpec((t, LANES), lambda b, h, i: (b * nq + i, h)),
            pl.BlockSpec((seq, LANES), lambda b, h, i: (b, h)),
            pl.BlockSpec((seq, DIFF_V_DIM), lambda b, h, i: (b, DIFF_HEADS + h)),
            vec, vec, vec, vec, _resident((1, DIFF_V_DIM)),
        ],
        out_specs=pl.BlockSpec((t, DIFF_V_DIM), lambda b, h, i: (b * nq + i, h)),
        out_shape=jax.ShapeDtypeStruct((n, TOK_WIDTH), BF16),
        compiler_params=_params(("parallel", "parallel", "arbitrary")),
        name="diff_attn",
    )(proj, kv, kv, lq1.reshape(1, -1), lk1.reshape(1, -1), lq2.reshape(1, -1), lk2.reshape(1, -1),
      subln.reshape(1, -1))


MIX_TM = 512


def _mix_out_kernel(x_ref, tok_ref, mq_ref, mkv_ref, w_ref, o_ref):
    mq = mq_ref[...] * jnp.asarray(MEM_HEAD_DIM ** -0.5, BF16)
    mkv = mkv_ref[...]
    acc = x_ref[...] + jnp.dot(tok_ref[...], w_ref[:TOK_WIDTH, :], preferred_element_type=F32)
    heads = []
    for h in range(MEM_HEADS):
        sl = slice(h * MEM_HEAD_DIM, (h + 1) * MEM_HEAD_DIM)
        s = lax.dot_general(mq[:, sl], mkv[:, sl], _NT, preferred_element_type=F32)
        p = jnp.exp(s - jnp.max(s, axis=1, keepdims=True))
        l = jnp.sum(p, axis=1, keepdims=True)
        v = mkv[:, MEM_WIDTH + h * MEM_HEAD_DIM:MEM_WIDTH + (h + 1) * MEM_HEAD_DIM]
        heads.append(jnp.dot(p.astype(BF16), v, preferred_element_type=F32) / l)
    mo = jnp.concatenate(heads, axis=1).astype(BF16)
    o_ref[...] = acc + jnp.dot(mo, w_ref[TOK_WIDTH:, :], preferred_element_type=F32)


def _mix_out(x, tok, proj, mkv, w, bsz, seq):
    n, d = x.shape
    tm = min(MIX_TM, seq)
    ns = seq // tm
    row = pl.BlockSpec((tm, d), lambda b, i: (b * ns + i, 0))
    return pl.pallas_call(
        _mix_out_kernel,
        grid=(bsz, ns),
        in_specs=[
            row,
            pl.BlockSpec((tm, TOK_WIDTH), lambda b, i: (b * ns + i, 0)),
            pl.BlockSpec((tm, MEM_WIDTH), lambda b, i: (b * ns + i, TOK_WIDTH // MEM_WIDTH)),
            pl.BlockSpec((N_MEM, 2 * MEM_WIDTH), lambda b, i: (b, 0)),
            _resident(w.shape),
        ],
        out_specs=row,
        out_shape=jax.ShapeDtypeStruct((n, d), F32),
        compiler_params=_params(("parallel", "parallel")),
        name="mix_out",
    )(x, tok, proj, mkv, w)


def kernel(x, mem, positions, ln_ffn1, ffn1_in, ffn1_out, ln_mix, w_mix_in, w_mix_out, ln_mem, w_mem_kv, ln_ffn2, ffn2_in, ffn2_out, s5_a_re, s5_a_im, s5_log_dt, s5_b_re, s5_b_im, s5_c_re, s5_c_im, s5_d, s5_w_glu, ln_kv, w_kv_shared, diff_lq1, diff_lk1, diff_lq2, diff_lk2, diff_subln, ln_final):
    bsz, seq, d = x.shape
    n = bsz * seq
    xf = x.reshape(n, d)
    memf = mem.reshape(bsz * N_MEM, d)
    rope = _rope_tables(positions)
    kv = None
    for i in range(DEPTH):
        if i == N_A:
            kv = _norm_proj(xf, ln_kv, w_kv_shared.astype(BF16), rope, QK_WIDTH)
        xf = _ffn(xf, ln_ffn1[i], ffn1_in[i].astype(BF16), ffn1_out[i].astype(BF16))
        if i < N_A:
            proj = _norm_proj(xf, ln_mix[i], w_mix_in[i].astype(BF16))
            tok = _s5_mixer(proj[:, :TOK_WIDTH], bsz, seq,
                            (s5_a_re[i], s5_a_im[i], s5_log_dt[i], s5_b_re[i], s5_b_im[i],
                             s5_c_re[i], s5_c_im[i], s5_d[i]), s5_w_glu[i].astype(BF16))
        else:
            j = i - N_A
            proj = _norm_proj(xf, ln_mix[i], w_mix_in[i].astype(BF16), rope, TOK_WIDTH)
            lam_init = 0.8 - 0.6 * math.exp(-0.3 * i)
            tok = _diff_attn(proj, kv, bsz, seq, diff_lq1[j], diff_lk1[j], diff_lq2[j], diff_lk2[j],
                             diff_subln[j], lam_init)
        mkv = _norm_proj(memf, ln_mem, w_mem_kv[i].astype(BF16))
        xf = _mix_out(xf, tok, proj, mkv, w_mix_out[i].astype(BF16), bsz, seq)
        xf = _ffn(xf, ln_ffn2[i], ffn2_in[i].astype(BF16), ffn2_out[i].astype(BF16),
                  final_g=ln_final if i == DEPTH - 1 else None)
    return xf.reshape(bsz, seq, d)
```

```python
import functools
import math

import jax
import jax.numpy as jnp
from jax import lax
from jax.experimental import pallas as pl
from jax.experimental.pallas import tpu as pltpu

F32 = jnp.float32
BF16 = jnp.bfloat16

D_MODEL = 1024
DEPTH = 4
N_A = DEPTH // 2
TOK_WIDTH = 768
MEM_HEADS = 4
MEM_HEAD_DIM = 64
MEM_WIDTH = MEM_HEADS * MEM_HEAD_DIM
N_MEM = 256
S5_GROUP = 16
S5_GROUPS = TOK_WIDTH // S5_GROUP
S5_STATE = 64
S5_CHUNK = 16
S5_FLAT = S5_CHUNK * S5_GROUP
DIFF_HEAD_DIM = 64
DIFF_HEADS = 6
DIFF_V_DIM = 128
QK_WIDTH = 768
ROT_DIM = 16
ROPE_THETA = 500000.0
D_FF = 2816
EPS = 1e-6

LANES = 128
VMEM_LIMIT = 56 * 1024 * 1024

_NT = (((1,), (1,)), ((), ()))


def _params(sem, vmem=VMEM_LIMIT):
    return pltpu.CompilerParams(dimension_semantics=sem, vmem_limit_bytes=vmem)


def _resident(shape):
    nd = len(shape)
    return pl.BlockSpec(shape, lambda *_: (0,) * nd, pipeline_mode=pl.Buffered(1))


def _rms(x, g):
    return x * lax.rsqrt(jnp.mean(x * x, axis=-1, keepdims=True) + EPS) * g


FFN_TM = 512
FFN_TF = 256


def _ffn_kernel(x_ref, g_ref, win_ref, wout_ref, *rest, final):
    if final:
        gf_ref, o_ref = rest
    else:
        (o_ref,) = rest
    x = x_ref[...]
    h = _rms(x, g_ref[...]).astype(BF16)
    acc = x
    for c in range(D_FF // FFN_TF):
        lo = c * FFN_TF
        gate = jnp.dot(h, win_ref[:, lo:lo + FFN_TF], preferred_element_type=F32)
        up = jnp.dot(h, win_ref[:, D_FF + lo:D_FF + lo + FFN_TF], preferred_element_type=F32)
        a = (gate * jax.nn.sigmoid(gate) * (0.5 * up)).astype(BF16)
        acc = acc + jnp.dot(a, wout_ref[lo:lo + FFN_TF, :], preferred_element_type=F32)
    if final:
        acc = _rms(acc, gf_ref[...])
    o_ref[...] = acc


def _ffn(x, g, w_in, w_out, final_g=None):
    n, d = x.shape
    row = pl.BlockSpec((FFN_TM, d), lambda i: (i, 0))
    in_specs = [row, _resident((1, d)), _resident(w_in.shape), _resident(w_out.shape)]
    args = [x, g.reshape(1, d), w_in, w_out]
    if final_g is not None:
        in_specs.append(_resident((1, d)))
        args.append(final_g.reshape(1, d))
    return pl.pallas_call(
        functools.partial(_ffn_kernel, final=final_g is not None),
        grid=(n // FFN_TM,),
        in_specs=in_specs,
        out_specs=row,
        out_shape=jax.ShapeDtypeStruct((n, d), F32),
        compiler_params=_params(("parallel",)),
        name="ffn",
    )(*args)


PROJ_TM = 512


def _rope_table_kernel(pos_ref, inv_ref, c_ref, sa_ref, sb_ref):
    ang = pos_ref[...].astype(F32) * inv_ref[...]
    lane = lax.broadcasted_iota(jnp.int32, ang.shape, 1) % DIFF_HEAD_DIM
    c = jnp.cos(ang)
    s = jnp.sin(ang)
    half = ROT_DIM // 2
    c_ref[...] = jnp.where(lane < ROT_DIM, c, 1.0)
    sa_ref[...] = jnp.where(lane < half, -s, 0.0)
    sb_ref[...] = jnp.where((lane >= half) & (lane < ROT_DIM), s, 0.0)


def _rope_tables(positions):
    n = positions.size
    half = ROT_DIM // 2
    inv = ROPE_THETA ** (-jnp.arange(0, ROT_DIM, 2, dtype=F32) / ROT_DIM)
    inv_lane = jnp.tile(inv, LANES // half).reshape(1, LANES)
    tm = 1024
    out = pl.BlockSpec((tm, LANES), lambda i: (i, 0))
    return pl.pallas_call(
        _rope_table_kernel,
        grid=(n // tm,),
        in_specs=[pl.BlockSpec((tm, 1), lambda i: (i, 0)), _resident((1, LANES))],
        out_specs=[out, out, out],
        out_shape=[jax.ShapeDtypeStruct((n, LANES), F32)] * 3,
        compiler_params=_params(("parallel",)),
        name="rope_tables",
    )(positions.reshape(n, 1), inv_lane)


def _norm_proj_kernel(x_ref, g_ref, w_ref, *rest, rope_cols, rope_scale, nat_cols, t_cols):
    h = _rms(x_ref[...], g_ref[...]).astype(BF16)
    y = jnp.dot(h, w_ref[...], preferred_element_type=F32)
    blocks = [y[:, t * LANES:(t + 1) * LANES] for t in range(y.shape[1] // LANES)]
    rest = list(rest)
    if rope_cols:
        c, sa, sb = (r[...] for r in rest[:3])
        rest = rest[3:]
        half = ROT_DIM // 2
        for t in range(rope_cols // LANES):
            blk = blocks[t]
            r = blk * c + pltpu.roll(blk, LANES - half, 1) * sa + pltpu.roll(blk, half, 1) * sb
            blocks[t] = r * rope_scale if rope_scale != 1.0 else r
    if nat_cols:
        o_ref = rest.pop(0)
        lo, hi = nat_cols
        for t in range(lo // LANES, hi // LANES):
            o_ref[:, t * LANES - lo:(t + 1) * LANES - lo] = blocks[t].astype(o_ref.dtype)
    if t_cols:
        ot_ref = rest.pop(0)
        lo, hi = t_cols
        for t in range(lo // LANES, hi // LANES):
            ot_ref[t * LANES - lo:(t + 1) * LANES - lo, :] = blocks[t].T.astype(ot_ref.dtype)


def _norm_proj(x, g, w, rope=None, rope_cols=0, rope_scale=1.0, nat_cols=None, t_cols=None,
               tm=PROJ_TM):
    n, d = x.shape
    nout = w.shape[1]
    tm = min(tm, n)
    if nat_cols is None and t_cols is None:
        nat_cols = (0, nout)
    in_specs = [pl.BlockSpec((tm, d), lambda i: (i, 0)), _resident((1, d)), _resident(w.shape)]
    args = [x, g.reshape(1, d), w]
    if rope_cols:
        tab = pl.BlockSpec((tm, LANES), lambda i: (i, 0))
        in_specs += [tab, tab, tab]
        args += list(rope)
    out_specs, out_shape = [], []
    if nat_cols:
        width = nat_cols[1] - nat_cols[0]
        out_specs.append(pl.BlockSpec((tm, width), lambda i: (i, 0)))
        out_shape.append(jax.ShapeDtypeStruct((n, width), BF16))
    if t_cols:
        width = t_cols[1] - t_cols[0]
        out_specs.append(pl.BlockSpec((None, width, tm), lambda i: (i, 0, 0)))
        out_shape.append(jax.ShapeDtypeStruct((n // tm, width, tm), BF16))
    out = pl.pallas_call(
        functools.partial(_norm_proj_kernel, rope_cols=rope_cols, rope_scale=rope_scale,
                          nat_cols=nat_cols, t_cols=t_cols),
        grid=(n // tm,),
        in_specs=in_specs,
        out_specs=out_specs,
        out_shape=out_shape,
        compiler_params=_params(("parallel",)),
        name="norm_proj",
    )(*args)
    return out[0] if len(out) == 1 else out


S5_POW_ROWS = 24


def _s5_prep_kernel(are_ref, aim_ref, ldt_ref, btr_ref, bti_ref, cr_ref, ci_ref, dd_ref,
                    t_ref, wend_ref, et_ref, da_ref, db_ref, e_sc, *, n_levels):
    lr = are_ref[...]
    li = aim_ref[...]
    dt = jnp.exp(ldt_ref[...])
    lane = lax.broadcasted_iota(jnp.int32, (1, LANES), 1)
    lo = lane < S5_STATE

    def powers(k):
        mag = jnp.exp(k * (lr * dt))
        ang = k * (li * dt)
        return mag * jnp.cos(ang), mag * jnp.sin(ang)

    kf = lax.broadcasted_iota(jnp.int32, (S5_POW_ROWS, 1), 0).astype(F32)
    pr, pi = powers(kf)
    p_ri = jnp.where(lo, pr, pi)
    p_mir = jnp.where(lo, -pi, pr)
    p_rmi = jnp.where(lo, pr, -pi)
    p_mimr = jnp.where(lo, -pi, -pr)

    abr, abi = pr[1:2], pi[1:2]
    den = lr * lr + li * li
    nr, ni = abr - 1.0, abi
    fr = (nr * lr + ni * li) / den
    fi = (ni * lr - nr * li) / den
    btr, bti = btr_ref[...], bti_ref[...]
    bbr = fr * btr - fi * bti
    bbi = fr * bti + fi * btr

    cr, ci = cr_ref[...], ci_ref[...]
    for k in range(S5_CHUNK + 1):
        e_sc[k * S5_GROUP:(k + 1) * S5_GROUP, :] = cr * p_rmi[k:k + 1] + ci * p_mimr[k:k + 1]
    et_ref[...] = e_sc[S5_GROUP:, :].astype(et_ref.dtype)

    for s in range(S5_CHUNK):
        k = S5_CHUNK - 1 - s
        w = bbr * p_ri[k:k + 1] + bbi * p_mir[k:k + 1]
        wend_ref[s * S5_GROUP:(s + 1) * S5_GROUP, :] = w.astype(wend_ref.dtype)

    bb = jnp.where(lo, bbr, bbi)
    kern = lax.dot_general(bb, e_sc[:S5_FLAT, :], _NT, precision=lax.Precision.HIGHEST,
                           preferred_element_type=F32)
    col = lax.broadcasted_iota(jnp.int32, (S5_GROUP, S5_FLAT), 1)
    row = lax.broadcasted_iota(jnp.int32, (S5_GROUP, S5_FLAT), 0)
    dd = dd_ref[...]
    for s in range(S5_CHUNK):
        shifted = kern if s == 0 else pltpu.roll(kern, s * S5_GROUP, 1)
        blk = jnp.where(col >= s * S5_GROUP, shifted, 0.0)
        blk = blk + jnp.where(col == row + s * S5_GROUP, dd, 0.0)
        t_ref[s * S5_GROUP:(s + 1) * S5_GROUP, :] = blk.astype(t_ref.dtype)

    steps = (S5_CHUNK * 2.0 ** lax.broadcasted_iota(jnp.int32, (n_levels, 1), 0).astype(F32))
    qr, qi = powers(steps)
    da_ref[...] = qr
    db_ref[...] = jnp.where(lo, -qi, qi)


def _dup(a):
    return jnp.concatenate([a, a], axis=-1)


def _s5_prep(a_re, a_im, log_dt, b_re, b_im, c_re, c_im, d, n_levels):
    g = S5_GROUPS
    args = [
        _dup(a_re).reshape(g, 1, LANES), _dup(a_im).reshape(g, 1, LANES), log_dt.reshape(g, 1, 1),
        _dup(jnp.swapaxes(b_re, 1, 2)), _dup(jnp.swapaxes(b_im, 1, 2)), _dup(c_re), _dup(c_im),
        jnp.tile(d, (1, S5_CHUNK)).reshape(g, 1, S5_FLAT),
    ]

    def per_group(*shape):
        return pl.BlockSpec((None,) + shape, lambda i: (i,) + (0,) * len(shape))

    in_specs = [per_group(1, LANES), per_group(1, LANES), per_group(1, 1),
                per_group(S5_GROUP, LANES), per_group(S5_GROUP, LANES),
                per_group(S5_GROUP, LANES), per_group(S5_GROUP, LANES), per_group(1, S5_FLAT)]
    out_specs = [per_group(S5_FLAT, S5_FLAT), per_group(S5_FLAT, LANES), per_group(S5_FLAT, LANES),
                 per_group(n_levels, LANES), per_group(n_levels, LANES)]
    out_shape = [jax.ShapeDtypeStruct((g, S5_FLAT, S5_FLAT), BF16),
                 jax.ShapeDtypeStruct((g, S5_FLAT, LANES), BF16),
                 jax.ShapeDtypeStruct((g, S5_FLAT, LANES), BF16),
                 jax.ShapeDtypeStruct((g, n_levels, LANES), F32),
                 jax.ShapeDtypeStruct((g, n_levels, LANES), F32)]
    return pl.pallas_call(
        functools.partial(_s5_prep_kernel, n_levels=n_levels),
        grid=(g,),
        in_specs=in_specs,
        out_specs=out_specs,
        out_shape=out_shape,
        scratch_shapes=[pltpu.VMEM(((S5_CHUNK + 1) * S5_GROUP, LANES), F32)],
        compiler_params=_params(("parallel",)),
        name="s5_prep",
    )(*args)


def _s5_main_kernel(uf_ref, t_ref, wend_ref, et_ref, da_ref, db_ref, y_ref, *, n_levels):
    nb, r, _ = uf_ref.shape
    u = uf_ref[...].reshape(nb * r, S5_FLAT)
    y = jnp.dot(u, t_ref[...], preferred_element_type=F32)
    hc = jnp.dot(u, wend_ref[...], preferred_element_type=F32)
    da, db = da_ref[...], db_ref[...]
    rows = lax.broadcasted_iota(jnp.int32, (r, LANES), 0)
    hprev = []
    for b in range(nb):
        x = hc[b * r:(b + 1) * r]
        for m in range(n_levels):
            d = 1 << m
            xs = jnp.where(rows >= d, pltpu.roll(x, d, 0), 0.0)
            x = x + xs * da[m:m + 1] + pltpu.roll(xs, S5_STATE, 1) * db[m:m + 1]
        hprev.append(jnp.where(rows >= 1, pltpu.roll(x, 1, 0), 0.0))
    hp = jnp.concatenate(hprev, axis=0).astype(BF16)
    y = y + lax.dot_general(hp, et_ref[...], _NT, preferred_element_type=F32)
    y_ref[...] = y.reshape(nb, r, S5_FLAT).astype(y_ref.dtype)


def _s5_main(uf, ops, n_levels):
    nb, g, r, _ = uf.shape
    t, wend, et, da, db = ops

    def per_group(*shape):
        return pl.BlockSpec((None,) + shape, lambda i: (i,) + (0,) * len(shape))

    io = pl.BlockSpec((nb, None, r, S5_FLAT), lambda i: (0, i, 0, 0))
    return pl.pallas_call(
        functools.partial(_s5_main_kernel, n_levels=n_levels),
        grid=(g,),
        in_specs=[io, per_group(S5_FLAT, S5_FLAT), per_group(S5_FLAT, LANES),
                  per_group(S5_FLAT, LANES), per_group(n_levels, LANES), per_group(n_levels, LANES)],
        out_specs=io,
        out_shape=jax.ShapeDtypeStruct(uf.shape, BF16),
        compiler_params=_params(("parallel",)),
        name="s5_main",
    )(uf, t, wend, et, da, db)


S5_POST_TM = 1024


def _s5_post_kernel(y_ref, w_ref, o_ref):
    z = jax.nn.gelu(y_ref[...].astype(F32), approximate=True)
    gate = jnp.dot(z.astype(BF16), w_ref[...], preferred_element_type=F32)
    o_ref[...] = (z * jax.nn.sigmoid(gate)).astype(o_ref.dtype)


def _s5_post(y, w_glu):
    n, w = y.shape
    tm = min(S5_POST_TM, n)
    row = pl.BlockSpec((tm, w), lambda i: (i, 0))
    return pl.pallas_call(
        _s5_post_kernel,
        grid=(n // tm,),
        in_specs=[row, _resident(w_glu.shape)],
        out_specs=row,
        out_shape=jax.ShapeDtypeStruct((n, w), BF16),
        compiler_params=_params(("parallel",)),
        name="s5_post",
    )(y, w_glu)


def _s5_mixer(u, bsz, seq, prep_args, w_glu):
    r = seq // S5_CHUNK
    n_levels = int(math.log2(r))
    assert 1 << n_levels == r
    ops = _s5_prep(*prep_args, n_levels=n_levels)
    uf = u.reshape(bsz, r, S5_CHUNK, S5_GROUPS, S5_GROUP).transpose(0, 3, 1, 2, 4)
    yf = _s5_main(uf.reshape(bsz, S5_GROUPS, r, S5_FLAT), ops, n_levels)
    y = yf.reshape(bsz, S5_GROUPS, r, S5_CHUNK, S5_GROUP).transpose(0, 2, 3, 1, 4)
    return _s5_post(y.reshape(bsz * seq, TOK_WIDTH), w_glu)


ATTN_T = 256
QK_SCALE = DIFF_HEAD_DIM ** -0.5 * math.log2(math.e)


ATTN_HB = 6
ATTN_SUM_ROWS = 16


def _diff_attn_kernel(qt_ref, k_ref, vt_ref, lq1_ref, lk1_ref, lq2_ref, lk2_ref, g_ref, o_ref,
                      *, lam_init):
    t = ATTN_T
    hw = 2 * DIFF_HEAD_DIM
    i = pl.program_id(2)
    qz = []
    for h in range(ATTN_HB):
        qt = qt_ref[h * hw:(h + 1) * hw, :].astype(F32)
        sub = lax.broadcasted_iota(jnp.int32, qt.shape, 0) // DIFF_HEAD_DIM
        qz.append(jnp.concatenate([jnp.where(sub == c, qt, 0.0) for c in range(2)],
                                  axis=1).astype(BF16))

    def step(j, carry, mask):
        kb = k_ref[pl.ds(pl.multiple_of(j * t, t), t), :]
        vtb = vt_ref[j]
        ss = [jnp.dot(kb[:, h * hw:(h + 1) * hw], qz[h], preferred_element_type=F32)
              for h in range(ATTN_HB)]
        ones = jnp.ones((ATTN_SUM_ROWS, t), BF16)
        out = []
        for h in range(ATTN_HB):
            m, acc = carry[h]
            s = ss[h] if mask is None else jnp.where(mask, ss[h], -jnp.inf)
            m_new = jnp.maximum(m, jnp.max(s, axis=0, keepdims=True))
            alpha = jnp.exp2(m - m_new)
            p = jnp.exp2(s - m_new).astype(BF16)
            v1 = jnp.concatenate([vtb[h * DIFF_V_DIM:(h + 1) * DIFF_V_DIM, :], ones], axis=0)
            pv = jnp.dot(v1, p, preferred_element_type=F32)
            out.append((m_new, alpha * acc + pv))
        return tuple(out)

    init = tuple((jnp.full((1, 2 * t), -1e30, F32),
                  jnp.zeros((DIFF_V_DIM + ATTN_SUM_ROWS, 2 * t), F32)) for _ in range(ATTN_HB))
    carry = lax.fori_loop(0, i, lambda j, c: step(j, c, None), init)
    key = lax.broadcasted_iota(jnp.int32, (t, 2 * t), 0)
    qry = lax.broadcasted_iota(jnp.int32, (t, 2 * t), 1) % t
    carry = step(i, carry, key <= qry)

    lam = (jnp.exp(jnp.sum(lq1_ref[...] * lk1_ref[...], axis=-1, keepdims=True))
           - jnp.exp(jnp.sum(lq2_ref[...] * lk2_ref[...], axis=-1, keepdims=True)) + lam_init)
    for h in range(ATTN_HB):
        _, acc = carry[h]
        r = acc[:DIFF_V_DIM] / acc[DIFF_V_DIM:DIFF_V_DIM + 1]
        o = r[:, :t] - lam * r[:, t:]
        o = o * lax.rsqrt(jnp.mean(o * o, axis=0, keepdims=True) + EPS) * g_ref[...] * (1.0 - lam_init)
        o_ref[:, h * DIFF_V_DIM:(h + 1) * DIFF_V_DIM] = o.T.astype(o_ref.dtype)


def _diff_attn(qt, k, vt, bsz, seq, lq1, lk1, lq2, lk2, subln, lam_init):
    t = ATTN_T
    n = k.shape[0]
    nq = seq // t
    hw = ATTN_HB * DIFF_V_DIM
    vec = _resident((1, DIFF_HEAD_DIM))
    return pl.pallas_call(
        functools.partial(_diff_attn_kernel, lam_init=lam_init),
        grid=(bsz, DIFF_HEADS // ATTN_HB, nq),
        in_specs=[
            pl.BlockSpec((None, hw, t), lambda b, h, i: (b * nq + i, h, 0)),
            pl.BlockSpec((seq, hw), lambda b, h, i: (b, h)),
            pl.BlockSpec((nq, hw, t), lambda b, h, i: (b, h, 0)),
            vec, vec, vec, vec, _resident((DIFF_V_DIM, 1)),
        ],
        out_specs=pl.BlockSpec((t, hw), lambda b, h, i: (b * nq + i, h)),
        out_shape=jax.ShapeDtypeStruct((n, TOK_WIDTH), BF16),
        compiler_params=_params(("parallel", "parallel", "arbitrary")),
        name="diff_attn",
    )(qt, k, vt, lq1.reshape(1, -1), lk1.reshape(1, -1), lq2.reshape(1, -1), lk2.reshape(1, -1),
      subln.reshape(-1, 1))


MIX_TM = 512


def _mix_out_kernel(x_ref, tok_ref, mq_ref, mkv_ref, w_ref, o_ref):
    mq = mq_ref[...] * jnp.asarray(MEM_HEAD_DIM ** -0.5, BF16)
    mkv = mkv_ref[...]
    acc = x_ref[...] + jnp.dot(tok_ref[...], w_ref[:TOK_WIDTH, :], preferred_element_type=F32)
    heads = []
    for h in range(MEM_HEADS):
        sl = slice(h * MEM_HEAD_DIM, (h + 1) * MEM_HEAD_DIM)
        s = lax.dot_general(mq[:, sl], mkv[:, sl], _NT, preferred_element_type=F32)
        p = jnp.exp(s - jnp.max(s, axis=1, keepdims=True))
        l = jnp.sum(p, axis=1, keepdims=True)
        v = mkv[:, MEM_WIDTH + h * MEM_HEAD_DIM:MEM_WIDTH + (h + 1) * MEM_HEAD_DIM]
        heads.append(jnp.dot(p.astype(BF16), v, preferred_element_type=F32) / l)
    mo = jnp.concatenate(heads, axis=1).astype(BF16)
    o_ref[...] = acc + jnp.dot(mo, w_ref[TOK_WIDTH:, :], preferred_element_type=F32)


def _mix_out(x, tok, mq, mkv, w, bsz, seq):
    n, d = x.shape
    tm = min(MIX_TM, seq)
    ns = seq // tm
    mq_block = mq.shape[1] // MEM_WIDTH - 1
    row = pl.BlockSpec((tm, d), lambda b, i: (b * ns + i, 0))
    return pl.pallas_call(
        _mix_out_kernel,
        grid=(bsz, ns),
        in_specs=[
            row,
            pl.BlockSpec((tm, TOK_WIDTH), lambda b, i: (b * ns + i, 0)),
            pl.BlockSpec((tm, MEM_WIDTH), lambda b, i: (b * ns + i, mq_block)),
            pl.BlockSpec((N_MEM, 2 * MEM_WIDTH), lambda b, i: (b, 0)),
            _resident(w.shape),
        ],
        out_specs=row,
        out_shape=jax.ShapeDtypeStruct((n, d), F32),
        compiler_params=_params(("parallel", "parallel")),
        name="mix_out",
    )(x, tok, mq, mkv, w)


def kernel(x, mem, positions, ln_ffn1, ffn1_in, ffn1_out, ln_mix, w_mix_in, w_mix_out, ln_mem, w_mem_kv, ln_ffn2, ffn2_in, ffn2_out, s5_a_re, s5_a_im, s5_log_dt, s5_b_re, s5_b_im, s5_c_re, s5_c_im, s5_d, s5_w_glu, ln_kv, w_kv_shared, diff_lq1, diff_lk1, diff_lq2, diff_lk2, diff_subln, ln_final):
    bsz, seq, d = x.shape
    n = bsz * seq
    xf = x.reshape(n, d)
    memf = mem.reshape(bsz * N_MEM, d)
    rope = _rope_tables(positions)
    k_sh = vt_sh = None
    for i in range(DEPTH):
        if i == N_A:
            k_sh, vt_sh = _norm_proj(xf, ln_kv, w_kv_shared.astype(BF16), rope, QK_WIDTH,
                                     nat_cols=(0, QK_WIDTH), t_cols=(QK_WIDTH, 2 * QK_WIDTH), tm=ATTN_T)
        xf = _ffn(xf, ln_ffn1[i], ffn1_in[i].astype(BF16), ffn1_out[i].astype(BF16))
        if i < N_A:
            mq = _norm_proj(xf, ln_mix[i], w_mix_in[i].astype(BF16))
            tok = _s5_mixer(mq[:, :TOK_WIDTH], bsz, seq,
                            (s5_a_re[i], s5_a_im[i], s5_log_dt[i], s5_b_re[i], s5_b_im[i],
                             s5_c_re[i], s5_c_im[i], s5_d[i]), s5_w_glu[i].astype(BF16))
        else:
            j = i - N_A
            mq, qt = _norm_proj(xf, ln_mix[i], w_mix_in[i].astype(BF16), rope, TOK_WIDTH, QK_SCALE,
                                nat_cols=(TOK_WIDTH, D_MODEL), t_cols=(0, TOK_WIDTH), tm=ATTN_T)
            lam_init = 0.8 - 0.6 * math.exp(-0.3 * i)
            tok = _diff_attn(qt, k_sh, vt_sh, bsz, seq, diff_lq1[j], diff_lk1[j], diff_lq2[j],
                             diff_lk2[j], diff_subln[j], lam_init)
        mkv = _norm_proj(memf, ln_mem, w_mem_kv[i].astype(BF16))
        xf = _mix_out(xf, tok, mq, mkv, w_mix_out[i].astype(BF16), bsz, seq)
        xf = _ffn(xf, ln_ffn2[i], ffn2_in[i].astype(BF16), ffn2_out[i].astype(BF16),
                  final_g=ln_final if i == DEPTH - 1 else None)
    return xf.reshape(bsz, seq, d)
```

```python
import functools
import math

import jax
import jax.numpy as jnp
from jax import lax
from jax.experimental import pallas as pl
from jax.experimental.pallas import tpu as pltpu

F32 = jnp.float32
BF16 = jnp.bfloat16

D_MODEL = 1024
DEPTH = 4
N_A = DEPTH // 2
TOK_WIDTH = 768
MEM_HEADS = 4
MEM_HEAD_DIM = 64
MEM_WIDTH = MEM_HEADS * MEM_HEAD_DIM
N_MEM = 256
S5_GROUP = 16
S5_GROUPS = TOK_WIDTH // S5_GROUP
S5_STATE = 64
S5_CHUNK = 16
S5_FLAT = S5_CHUNK * S5_GROUP
DIFF_HEAD_DIM = 64
DIFF_HEADS = 6
DIFF_V_DIM = 128
QK_WIDTH = 768
ROT_DIM = 16
ROPE_THETA = 500000.0
D_FF = 2816
EPS = 1e-6

LANES = 128
VMEM_LIMIT = 56 * 1024 * 1024

_NT = (((1,), (1,)), ((), ()))


def _params(sem, vmem=VMEM_LIMIT):
    return pltpu.CompilerParams(dimension_semantics=sem, vmem_limit_bytes=vmem)


def _resident(shape):
    nd = len(shape)
    return pl.BlockSpec(shape, lambda *_: (0,) * nd, pipeline_mode=pl.Buffered(1))


def _rms(x, g):
    return x * lax.rsqrt(jnp.mean(x * x, axis=-1, keepdims=True) + EPS) * g


FFN_TM = 512
FFN_TF = 256


def _ffn_kernel(x_ref, g_ref, win_ref, wout_ref, *rest, final):
    if final:
        gf_ref, o_ref = rest
    else:
        (o_ref,) = rest
    x = x_ref[...]
    h = _rms(x, g_ref[...]).astype(BF16)
    acc = x
    for c in range(D_FF // FFN_TF):
        lo = c * FFN_TF
        gate = jnp.dot(h, win_ref[:, lo:lo + FFN_TF], preferred_element_type=F32)
        up = jnp.dot(h, win_ref[:, D_FF + lo:D_FF + lo + FFN_TF], preferred_element_type=F32)
        a = (gate * jax.nn.sigmoid(gate) * (0.5 * up)).astype(BF16)
        acc = acc + jnp.dot(a, wout_ref[lo:lo + FFN_TF, :], preferred_element_type=F32)
    if final:
        acc = _rms(acc, gf_ref[...])
    o_ref[...] = acc


def _ffn(x, g, w_in, w_out, final_g=None):
    n, d = x.shape
    row = pl.BlockSpec((FFN_TM, d), lambda i: (i, 0))
    in_specs = [row, _resident((1, d)), _resident(w_in.shape), _resident(w_out.shape)]
    args = [x, g.reshape(1, d), w_in, w_out]
    if final_g is not None:
        in_specs.append(_resident((1, d)))
        args.append(final_g.reshape(1, d))
    return pl.pallas_call(
        functools.partial(_ffn_kernel, final=final_g is not None),
        grid=(n // FFN_TM,),
        in_specs=in_specs,
        out_specs=row,
        out_shape=jax.ShapeDtypeStruct((n, d), F32),
        compiler_params=_params(("parallel",)),
        name="ffn",
    )(*args)


PROJ_TM = 512


def _rope_table_kernel(pos_ref, inv_ref, c_ref, sa_ref, sb_ref):
    ang = pos_ref[...].astype(F32) * inv_ref[...]
    lane = lax.broadcasted_iota(jnp.int32, ang.shape, 1) % DIFF_HEAD_DIM
    c = jnp.cos(ang)
    s = jnp.sin(ang)
    half = ROT_DIM // 2
    c_ref[...] = jnp.where(lane < ROT_DIM, c, 1.0)
    sa_ref[...] = jnp.where(lane < half, -s, 0.0)
    sb_ref[...] = jnp.where((lane >= half) & (lane < ROT_DIM), s, 0.0)


def _rope_tables(positions):
    n = positions.size
    half = ROT_DIM // 2
    inv = ROPE_THETA ** (-jnp.arange(0, ROT_DIM, 2, dtype=F32) / ROT_DIM)
    inv_lane = jnp.tile(inv, LANES // half).reshape(1, LANES)
    tm = 1024
    out = pl.BlockSpec((tm, LANES), lambda i: (i, 0))
    return pl.pallas_call(
        _rope_table_kernel,
        grid=(n // tm,),
        in_specs=[pl.BlockSpec((tm, 1), lambda i: (i, 0)), _resident((1, LANES))],
        out_specs=[out, out, out],
        out_shape=[jax.ShapeDtypeStruct((n, LANES), F32)] * 3,
        compiler_params=_params(("parallel",)),
        name="rope_tables",
    )(positions.reshape(n, 1), inv_lane)


def _norm_proj_kernel(x_ref, g_ref, w_ref, *rest, rope_cols, rope_scale, nat_cols, t_cols, sm_cols):
    h = _rms(x_ref[...], g_ref[...]).astype(BF16)
    y = jnp.dot(h, w_ref[...], preferred_element_type=F32)
    blocks = [y[:, t * LANES:(t + 1) * LANES] for t in range(y.shape[1] // LANES)]
    rest = list(rest)
    if rope_cols:
        c, sa, sb = (r[...] for r in rest[:3])
        rest = rest[3:]
        half = ROT_DIM // 2
        for t in range(rope_cols // LANES):
            blk = blocks[t]
            r = blk * c + pltpu.roll(blk, LANES - half, 1) * sa + pltpu.roll(blk, half, 1) * sb
            blocks[t] = r * rope_scale if rope_scale != 1.0 else r
    if nat_cols:
        o_ref = rest.pop(0)
        lo, hi = nat_cols
        for t in range(lo // LANES, hi // LANES):
            o_ref[:, t * LANES - lo:(t + 1) * LANES - lo] = blocks[t].astype(o_ref.dtype)
    if t_cols:
        ot_ref = rest.pop(0)
        lo, hi = t_cols
        for t in range(lo // LANES, hi // LANES):
            ot_ref[t * LANES - lo:(t + 1) * LANES - lo, :] = blocks[t].T.astype(ot_ref.dtype)
    if sm_cols:
        os_ref, sm_sc = rest
        lo, hi = sm_cols
        for t in range(lo // LANES, hi // LANES):
            sm_sc[t - lo // LANES] = blocks[t]
            for s in range(S5_CHUNK):
                os_ref[s, :, t * LANES - lo:(t + 1) * LANES - lo] = sm_sc[
                    t - lo // LANES, pl.ds(s, os_ref.shape[1], stride=S5_CHUNK), :].astype(os_ref.dtype)


def _norm_proj(x, g, w, rope=None, rope_cols=0, rope_scale=1.0, nat_cols=None, t_cols=None,
               sm_cols=None, tm=PROJ_TM):
    n, d = x.shape
    nout = w.shape[1]
    tm = min(tm, n)
    if nat_cols is None and t_cols is None and sm_cols is None:
        nat_cols = (0, nout)
    in_specs = [pl.BlockSpec((tm, d), lambda i: (i, 0)), _resident((1, d)), _resident(w.shape)]
    args = [x, g.reshape(1, d), w]
    if rope_cols:
        tab = pl.BlockSpec((tm, LANES), lambda i: (i, 0))
        in_specs += [tab, tab, tab]
        args += list(rope)
    out_specs, out_shape = [], []
    if nat_cols:
        width = nat_cols[1] - nat_cols[0]
        out_specs.append(pl.BlockSpec((tm, width), lambda i: (i, 0)))
        out_shape.append(jax.ShapeDtypeStruct((n, width), BF16))
    if t_cols:
        width = t_cols[1] - t_cols[0]
        out_specs.append(pl.BlockSpec((None, width, tm), lambda i: (i, 0, 0)))
        out_shape.append(jax.ShapeDtypeStruct((n // tm, width, tm), BF16))
    scratch = []
    if sm_cols:
        width = sm_cols[1] - sm_cols[0]
        out_specs.append(pl.BlockSpec((S5_CHUNK, tm // S5_CHUNK, width), lambda i: (0, i, 0)))
        out_shape.append(jax.ShapeDtypeStruct((S5_CHUNK, n // S5_CHUNK, width), BF16))
        scratch.append(pltpu.VMEM((width // LANES, tm, LANES), F32))
    out = pl.pallas_call(
        functools.partial(_norm_proj_kernel, rope_cols=rope_cols, rope_scale=rope_scale,
                          nat_cols=nat_cols, t_cols=t_cols, sm_cols=sm_cols),
        grid=(n // tm,),
        in_specs=in_specs,
        out_specs=out_specs,
        out_shape=out_shape,
        scratch_shapes=scratch,
        compiler_params=_params(("parallel",)),
        name="norm_proj",
    )(*args)
    return out[0] if len(out) == 1 else out


S5_POW_ROWS = 24


def _s5_prep_kernel(are_ref, aim_ref, ldt_ref, btr_ref, bti_ref, cr_ref, ci_ref, dd_ref,
                    t_ref, wend_ref, et_ref, da_ref, db_ref, e_sc, *, n_levels):
    lr = are_ref[...]
    li = aim_ref[...]
    dt = jnp.exp(ldt_ref[...])
    lane = lax.broadcasted_iota(jnp.int32, (1, LANES), 1)
    lo = lane < S5_STATE

    def powers(k):
        mag = jnp.exp(k * (lr * dt))
        ang = k * (li * dt)
        return mag * jnp.cos(ang), mag * jnp.sin(ang)

    kf = lax.broadcasted_iota(jnp.int32, (S5_POW_ROWS, 1), 0).astype(F32)
    pr, pi = powers(kf)
    p_ri = jnp.where(lo, pr, pi)
    p_mir = jnp.where(lo, -pi, pr)
    p_rmi = jnp.where(lo, pr, -pi)
    p_mimr = jnp.where(lo, -pi, -pr)

    abr, abi = pr[1:2], pi[1:2]
    den = lr * lr + li * li
    nr, ni = abr - 1.0, abi
    fr = (nr * lr + ni * li) / den
    fi = (ni * lr - nr * li) / den
    btr, bti = btr_ref[...], bti_ref[...]
    bbr = fr * btr - fi * bti
    bbi = fr * bti + fi * btr

    cr, ci = cr_ref[...], ci_ref[...]
    for k in range(S5_CHUNK + 1):
        e_sc[k * S5_GROUP:(k + 1) * S5_GROUP, :] = cr * p_rmi[k:k + 1] + ci * p_mimr[k:k + 1]
    et_ref[...] = e_sc[S5_GROUP:, :].astype(et_ref.dtype)

    for s in range(S5_CHUNK):
        k = S5_CHUNK - 1 - s
        w = bbr * p_ri[k:k + 1] + bbi * p_mir[k:k + 1]
        wend_ref[s * S5_GROUP:(s + 1) * S5_GROUP, :] = w.astype(wend_ref.dtype)

    bb = jnp.where(lo, bbr, bbi)
    kern = lax.dot_general(bb, e_sc[:S5_FLAT, :], _NT, precision=lax.Precision.HIGHEST,
                           preferred_element_type=F32)
    col = lax.broadcasted_iota(jnp.int32, (S5_GROUP, S5_FLAT), 1)
    row = lax.broadcasted_iota(jnp.int32, (S5_GROUP, S5_FLAT), 0)
    dd = dd_ref[...]
    for s in range(S5_CHUNK):
        shifted = kern if s == 0 else pltpu.roll(kern, s * S5_GROUP, 1)
        blk = jnp.where(col >= s * S5_GROUP, shifted, 0.0)
        blk = blk + jnp.where(col == row + s * S5_GROUP, dd, 0.0)
        t_ref[s * S5_GROUP:(s + 1) * S5_GROUP, :] = blk.astype(t_ref.dtype)

    steps = (S5_CHUNK * 2.0 ** lax.broadcasted_iota(jnp.int32, (n_levels, 1), 0).astype(F32))
    qr, qi = powers(steps)
    da_ref[...] = qr
    db_ref[...] = jnp.where(lo, -qi, qi)


def _dup(a):
    return jnp.concatenate([a, a], axis=-1)


def _s5_prep(a_re, a_im, log_dt, b_re, b_im, c_re, c_im, d, n_levels):
    g = S5_GROUPS
    args = [
        _dup(a_re).reshape(g, 1, LANES), _dup(a_im).reshape(g, 1, LANES), log_dt.reshape(g, 1, 1),
        _dup(jnp.swapaxes(b_re, 1, 2)), _dup(jnp.swapaxes(b_im, 1, 2)), _dup(c_re), _dup(c_im),
        jnp.tile(d, (1, S5_CHUNK)).reshape(g, 1, S5_FLAT),
    ]

    def per_group(*shape):
        return pl.BlockSpec((None,) + shape, lambda i: (i,) + (0,) * len(shape))

    in_specs = [per_group(1, LANES), per_group(1, LANES), per_group(1, 1),
                per_group(S5_GROUP, LANES), per_group(S5_GROUP, LANES),
                per_group(S5_GROUP, LANES), per_group(S5_GROUP, LANES), per_group(1, S5_FLAT)]
    out_specs = [per_group(S5_FLAT, S5_FLAT), per_group(S5_FLAT, LANES), per_group(S5_FLAT, LANES),
                 per_group(n_levels, LANES), per_group(n_levels, LANES)]
    out_shape = [jax.ShapeDtypeStruct((g, S5_FLAT, S5_FLAT), BF16),
                 jax.ShapeDtypeStruct((g, S5_FLAT, LANES), BF16),
                 jax.ShapeDtypeStruct((g, S5_FLAT, LANES), BF16),
                 jax.ShapeDtypeStruct((g, n_levels, LANES), F32),
                 jax.ShapeDtypeStruct((g, n_levels, LANES), F32)]
    return pl.pallas_call(
        functools.partial(_s5_prep_kernel, n_levels=n_levels),
        grid=(g,),
        in_specs=in_specs,
        out_specs=out_specs,
        out_shape=out_shape,
        scratch_shapes=[pltpu.VMEM(((S5_CHUNK + 1) * S5_GROUP, LANES), F32)],
        compiler_params=_params(("parallel",)),
        name="s5_prep",
    )(*args)


S5_GPB = LANES // S5_GROUP


def _s5_main_kernel(us_ref, t_ref, wend_ref, et_ref, da_ref, db_ref, ys_ref, y_sc, *, nb, n_levels):
    r = us_ref.shape[1] // nb
    rows = lax.broadcasted_iota(jnp.int32, (r, LANES), 0)
    piece = lax.broadcasted_iota(jnp.int32, (r, LANES), 1) // S5_GROUP

    def gather(srcs):
        out = None
        for p, (src, src_piece) in enumerate(srcs):
            shift = (S5_GROUP * (p - src_piece)) % LANES
            v = src if shift == 0 else pltpu.roll(src, shift, 1)
            out = v if out is None else jnp.where(piece == p, v, out)
        return out

    def per_batch(b, carry):
        row0 = pl.multiple_of(b * r, r)
        xs = [us_ref[s, pl.ds(row0, r), :].astype(F32) for s in range(S5_CHUNK)]
        for gl in range(S5_GPB):
            u = jnp.concatenate(
                [gather([(xs[S5_GPB * hf + sp], gl) for sp in range(S5_GPB)]) for hf in range(2)],
                axis=1).astype(BF16)
            y = jnp.dot(u, t_ref[gl], preferred_element_type=F32)
            x = jnp.dot(u, wend_ref[gl], preferred_element_type=F32)
            da, db = da_ref[gl], db_ref[gl]
            for m in range(n_levels):
                d = 1 << m
                sh = jnp.where(rows >= d, pltpu.roll(x, d, 0), 0.0)
                x = x + sh * da[m:m + 1] + pltpu.roll(sh, S5_STATE, 1) * db[m:m + 1]
            hp = jnp.where(rows >= 1, pltpu.roll(x, 1, 0), 0.0).astype(BF16)
            y_sc[gl] = y + lax.dot_general(hp, et_ref[gl], _NT, preferred_element_type=F32)
        for t in range(S5_CHUNK):
            hf, tp = divmod(t, S5_GPB)
            out = gather([(y_sc[gl, :, hf * LANES:(hf + 1) * LANES], tp) for gl in range(S5_GPB)])
            ys_ref[t, pl.ds(row0, r), :] = out.astype(ys_ref.dtype)
        return carry

    lax.fori_loop(0, nb, per_batch, 0)


def _s5_main(us, ops, nb, n_levels):
    _, rows, width = us.shape
    t, wend, et, da, db = ops

    def per_block(*shape):
        return pl.BlockSpec((S5_GPB,) + shape, lambda i: (i,) + (0,) * len(shape))

    io = pl.BlockSpec((S5_CHUNK, rows, LANES), lambda i: (0, 0, i))
    return pl.pallas_call(
        functools.partial(_s5_main_kernel, nb=nb, n_levels=n_levels),
        grid=(width // LANES,),
        in_specs=[io, per_block(S5_FLAT, S5_FLAT), per_block(S5_FLAT, LANES),
                  per_block(S5_FLAT, LANES), per_block(n_levels, LANES), per_block(n_levels, LANES)],
        out_specs=io,
        out_shape=jax.ShapeDtypeStruct(us.shape, BF16),
        scratch_shapes=[pltpu.VMEM((S5_GPB, rows // nb, S5_FLAT), F32)],
        compiler_params=_params(("parallel",)),
        name="s5_main",
    )(us, t, wend, et, da, db)


S5_POST_TM = 1024


def _s5_post_kernel(y_ref, w_ref, o_ref):
    z = jax.nn.gelu(y_ref[...].astype(F32), approximate=True)
    gate = jnp.dot(z.astype(BF16), w_ref[...], preferred_element_type=F32)
    o_ref[...] = (z * jax.nn.sigmoid(gate)).astype(o_ref.dtype)


def _s5_post(y, w_glu):
    n, w = y.shape
    tm = min(S5_POST_TM, n)
    row = pl.BlockSpec((tm, w), lambda i: (i, 0))
    return pl.pallas_call(
        _s5_post_kernel,
        grid=(n // tm,),
        in_specs=[row, _resident(w_glu.shape)],
        out_specs=row,
        out_shape=jax.ShapeDtypeStruct((n, w), BF16),
        compiler_params=_params(("parallel",)),
        name="s5_post",
    )(y, w_glu)


def _s5_mixer(us, bsz, seq, prep_args, w_glu):
    r = seq // S5_CHUNK
    n_levels = int(math.log2(r))
    assert 1 << n_levels == r
    ops = _s5_prep(*prep_args, n_levels=n_levels)
    ys = _s5_main(us, ops, bsz, n_levels)
    return _s5_post(ys.reshape(bsz * seq, TOK_WIDTH), w_glu).reshape(us.shape)


ATTN_T = 256
QK_SCALE = DIFF_HEAD_DIM ** -0.5 * math.log2(math.e)


ATTN_HB = 6
ATTN_SUM_ROWS = 16


def _diff_attn_kernel(qt_ref, k_ref, vt_ref, lq1_ref, lk1_ref, lq2_ref, lk2_ref, g_ref, o_ref,
                      *, lam_init):
    t = ATTN_T
    hw = 2 * DIFF_HEAD_DIM
    i = pl.program_id(2)
    qz = []
    for h in range(ATTN_HB):
        qt = qt_ref[h * hw:(h + 1) * hw, :].astype(F32)
        sub = lax.broadcasted_iota(jnp.int32, qt.shape, 0) // DIFF_HEAD_DIM
        qz.append(jnp.concatenate([jnp.where(sub == c, qt, 0.0) for c in range(2)],
                                  axis=1).astype(BF16))

    def step(j, carry, mask):
        kb = k_ref[pl.ds(pl.multiple_of(j * t, t), t), :]
        vtb = vt_ref[j]
        ss = [jnp.dot(kb[:, h * hw:(h + 1) * hw], qz[h], preferred_element_type=F32)
              for h in range(ATTN_HB)]
        ones = jnp.ones((ATTN_SUM_ROWS, t), BF16)
        out = []
        for h in range(ATTN_HB):
            m, acc = carry[h]
            s = ss[h] if mask is None else jnp.where(mask, ss[h], -jnp.inf)
            m_new = jnp.maximum(m, jnp.max(s, axis=0, keepdims=True))
            alpha = jnp.exp2(m - m_new)
            p = jnp.exp2(s - m_new).astype(BF16)
            v1 = jnp.concatenate([vtb[h * DIFF_V_DIM:(h + 1) * DIFF_V_DIM, :], ones], axis=0)
            pv = jnp.dot(v1, p, preferred_element_type=F32)
            out.append((m_new, alpha * acc + pv))
        return tuple(out)

    init = tuple((jnp.full((1, 2 * t), -1e30, F32),
                  jnp.zeros((DIFF_V_DIM + ATTN_SUM_ROWS, 2 * t), F32)) for _ in range(ATTN_HB))
    carry = lax.fori_loop(0, i, lambda j, c: step(j, c, None), init)
    key = lax.broadcasted_iota(jnp.int32, (t, 2 * t), 0)
    qry = lax.broadcasted_iota(jnp.int32, (t, 2 * t), 1) % t
    carry = step(i, carry, key <= qry)

    lam = (jnp.exp(jnp.sum(lq1_ref[...] * lk1_ref[...], axis=-1, keepdims=True))
           - jnp.exp(jnp.sum(lq2_ref[...] * lk2_ref[...], axis=-1, keepdims=True)) + lam_init)
    for h in range(ATTN_HB):
        _, acc = carry[h]
        r = acc[:DIFF_V_DIM] / acc[DIFF_V_DIM:DIFF_V_DIM + 1]
        o = r[:, :t] - lam * r[:, t:]
        o = o * lax.rsqrt(jnp.mean(o * o, axis=0, keepdims=True) + EPS) * g_ref[...] * (1.0 - lam_init)
        o_ref[:, h * DIFF_V_DIM:(h + 1) * DIFF_V_DIM] = o.T.astype(o_ref.dtype)


def _diff_attn(qt, k, vt, bsz, seq, lq1, lk1, lq2, lk2, subln, lam_init):
    t = ATTN_T
    n = k.shape[0]
    nq = seq // t
    hw = ATTN_HB * DIFF_V_DIM
    vec = _resident((1, DIFF_HEAD_DIM))
    return pl.pallas_call(
        functools.partial(_diff_attn_kernel, lam_init=lam_init),
        grid=(bsz, DIFF_HEADS // ATTN_HB, nq),
        in_specs=[
            pl.BlockSpec((None, hw, t), lambda b, h, i: (b * nq + i, h, 0)),
            pl.BlockSpec((seq, hw), lambda b, h, i: (b, h)),
            pl.BlockSpec((nq, hw, t), lambda b, h, i: (b, h, 0)),
            vec, vec, vec, vec, _resident((DIFF_V_DIM, 1)),
        ],
        out_specs=pl.BlockSpec((t, hw), lambda b, h, i: (b * nq + i, h)),
        out_shape=jax.ShapeDtypeStruct((n, TOK_WIDTH), BF16),
        compiler_params=_params(("parallel", "parallel", "arbitrary")),
        name="diff_attn",
    )(qt, k, vt, lq1.reshape(1, -1), lk1.reshape(1, -1), lq2.reshape(1, -1), lk2.reshape(1, -1),
      subln.reshape(-1, 1))


MIX_TM = 512


def _mix_out_kernel(x_ref, tok_ref, mq_ref, mkv_ref, w_ref, o_ref, *scratch):
    if scratch:
        (tok_sc,) = scratch
        for t in range(TOK_WIDTH // LANES):
            for s in range(S5_CHUNK):
                tok_sc[t, pl.ds(s, tok_ref.shape[1], stride=S5_CHUNK), :] = (
                    tok_ref[s, :, t * LANES:(t + 1) * LANES].astype(F32))
        tok = jnp.concatenate([tok_sc[t] for t in range(TOK_WIDTH // LANES)], axis=1).astype(BF16)
    else:
        tok = tok_ref[...]
    mq = mq_ref[...] * jnp.asarray(MEM_HEAD_DIM ** -0.5, BF16)
    mkv = mkv_ref[...]
    acc = x_ref[...] + jnp.dot(tok, w_ref[:TOK_WIDTH, :], preferred_element_type=F32)
    heads = []
    for h in range(MEM_HEADS):
        sl = slice(h * MEM_HEAD_DIM, (h + 1) * MEM_HEAD_DIM)
        s = lax.dot_general(mq[:, sl], mkv[:, sl], _NT, preferred_element_type=F32)
        p = jnp.exp(s - jnp.max(s, axis=1, keepdims=True))
        l = jnp.sum(p, axis=1, keepdims=True)
        v = mkv[:, MEM_WIDTH + h * MEM_HEAD_DIM:MEM_WIDTH + (h + 1) * MEM_HEAD_DIM]
        heads.append(jnp.dot(p.astype(BF16), v, preferred_element_type=F32) / l)
    mo = jnp.concatenate(heads, axis=1).astype(BF16)
    o_ref[...] = acc + jnp.dot(mo, w_ref[TOK_WIDTH:, :], preferred_element_type=F32)


def _mix_out(x, tok, mq, mkv, w, bsz, seq):
    n, d = x.shape
    tm = min(MIX_TM, seq)
    ns = seq // tm
    mq_block = mq.shape[1] // MEM_WIDTH - 1
    row = pl.BlockSpec((tm, d), lambda b, i: (b * ns + i, 0))
    if tok.ndim == 3:
        tok_spec = pl.BlockSpec((S5_CHUNK, tm // S5_CHUNK, TOK_WIDTH), lambda b, i: (0, b * ns + i, 0))
        scratch = [pltpu.VMEM((TOK_WIDTH // LANES, tm, LANES), F32)]
    else:
        tok_spec = pl.BlockSpec((tm, TOK_WIDTH), lambda b, i: (b * ns + i, 0))
        scratch = []
    return pl.pallas_call(
        _mix_out_kernel,
        grid=(bsz, ns),
        in_specs=[
            row,
            tok_spec,
            pl.BlockSpec((tm, MEM_WIDTH), lambda b, i: (b * ns + i, mq_block)),
            pl.BlockSpec((N_MEM, 2 * MEM_WIDTH), lambda b, i: (b, 0)),
            _resident(w.shape),
        ],
        out_specs=row,
        out_shape=jax.ShapeDtypeStruct((n, d), F32),
        scratch_shapes=scratch,
        compiler_params=_params(("parallel", "parallel")),
        name="mix_out",
    )(x, tok, mq, mkv, w)


def kernel(x, mem, positions, ln_ffn1, ffn1_in, ffn1_out, ln_mix, w_mix_in, w_mix_out, ln_mem, w_mem_kv, ln_ffn2, ffn2_in, ffn2_out, s5_a_re, s5_a_im, s5_log_dt, s5_b_re, s5_b_im, s5_c_re, s5_c_im, s5_d, s5_w_glu, ln_kv, w_kv_shared, diff_lq1, diff_lk1, diff_lq2, diff_lk2, diff_subln, ln_final):
    bsz, seq, d = x.shape
    n = bsz * seq
    xf = x.reshape(n, d)
    memf = mem.reshape(bsz * N_MEM, d)
    rope = _rope_tables(positions)
    k_sh = vt_sh = None
    for i in range(DEPTH):
        if i == N_A:
            k_sh, vt_sh = _norm_proj(xf, ln_kv, w_kv_shared.astype(BF16), rope, QK_WIDTH,
                                     nat_cols=(0, QK_WIDTH), t_cols=(QK_WIDTH, 2 * QK_WIDTH), tm=ATTN_T)
        xf = _ffn(xf, ln_ffn1[i], ffn1_in[i].astype(BF16), ffn1_out[i].astype(BF16))
        if i < N_A:
            mq, us = _norm_proj(xf, ln_mix[i], w_mix_in[i].astype(BF16),
                                nat_cols=(TOK_WIDTH, D_MODEL), sm_cols=(0, TOK_WIDTH))
            tok = _s5_mixer(us, bsz, seq,
                            (s5_a_re[i], s5_a_im[i], s5_log_dt[i], s5_b_re[i], s5_b_im[i],
                             s5_c_re[i], s5_c_im[i], s5_d[i]), s5_w_glu[i].astype(BF16))
        else:
            j = i - N_A
            mq, qt = _norm_proj(xf, ln_mix[i], w_mix_in[i].astype(BF16), rope, TOK_WIDTH, QK_SCALE,
                                nat_cols=(TOK_WIDTH, D_MODEL), t_cols=(0, TOK_WIDTH), tm=ATTN_T)
            lam_init = 0.8 - 0.6 * math.exp(-0.3 * i)
            tok = _diff_attn(qt, k_sh, vt_sh, bsz, seq, diff_lq1[j], diff_lk1[j], diff_lq2[j],
                             diff_lk2[j], diff_subln[j], lam_init)
        mkv = _norm_proj(memf, ln_mem, w_mem_kv[i].astype(BF16))
        xf = _mix_out(xf, tok, mq, mkv, w_mix_out[i].astype(BF16), bsz, seq)
        xf = _ffn(xf, ln_ffn2[i], ffn2_in[i].astype(BF16), ffn2_out[i].astype(BF16),
                  final_g=ln_final if i == DEPTH - 1 else None)
    return xf.reshape(bsz, seq, d)
```

```python
import functools
import math

import jax
import jax.numpy as jnp
from jax import lax
from jax.experimental import pallas as pl
from jax.experimental.pallas import tpu as pltpu

F32 = jnp.float32
BF16 = jnp.bfloat16

D_MODEL = 1024
DEPTH = 4
N_A = DEPTH // 2
TOK_WIDTH = 768
MEM_HEADS = 4
MEM_HEAD_DIM = 64
MEM_WIDTH = MEM_HEADS * MEM_HEAD_DIM
N_MEM = 256
S5_GROUP = 16
S5_GROUPS = TOK_WIDTH // S5_GROUP
S5_STATE = 64
S5_CHUNK = 16
S5_FLAT = S5_CHUNK * S5_GROUP
DIFF_HEAD_DIM = 64
DIFF_HEADS = 6
DIFF_V_DIM = 128
QK_WIDTH = 768
ROT_DIM = 16
ROPE_THETA = 500000.0
D_FF = 2816
EPS = 1e-6

LANES = 128
VMEM_LIMIT = 56 * 1024 * 1024

_NT = (((1,), (1,)), ((), ()))


def _params(sem, vmem=VMEM_LIMIT):
    return pltpu.CompilerParams(dimension_semantics=sem, vmem_limit_bytes=vmem)


def _resident(shape):
    nd = len(shape)
    return pl.BlockSpec(shape, lambda *_: (0,) * nd, pipeline_mode=pl.Buffered(1))


def _rms(x, g):
    return x * lax.rsqrt(jnp.mean(x * x, axis=-1, keepdims=True) + EPS) * g


FFN_TM = 512
FFN_TF = 256


def _ffn_kernel(x_ref, g_ref, win_ref, wout_ref, *rest, final):
    if final:
        gf_ref, o_ref = rest
    else:
        (o_ref,) = rest
    x = x_ref[...]
    h = _rms(x, g_ref[...]).astype(BF16)
    acc = x
    for c in range(D_FF // FFN_TF):
        lo = c * FFN_TF
        gate = jnp.dot(h, win_ref[:, lo:lo + FFN_TF], preferred_element_type=F32)
        up = jnp.dot(h, win_ref[:, D_FF + lo:D_FF + lo + FFN_TF], preferred_element_type=F32)
        a = (gate * jax.nn.sigmoid(gate) * (0.5 * up)).astype(BF16)
        acc = acc + jnp.dot(a, wout_ref[lo:lo + FFN_TF, :], preferred_element_type=F32)
    if final:
        acc = _rms(acc, gf_ref[...])
    o_ref[...] = acc


def _ffn(x, g, w_in, w_out, final_g=None):
    n, d = x.shape
    row = pl.BlockSpec((FFN_TM, d), lambda i: (i, 0))
    in_specs = [row, _resident((1, d)), _resident(w_in.shape), _resident(w_out.shape)]
    args = [x, g.reshape(1, d), w_in, w_out]
    if final_g is not None:
        in_specs.append(_resident((1, d)))
        args.append(final_g.reshape(1, d))
    return pl.pallas_call(
        functools.partial(_ffn_kernel, final=final_g is not None),
        grid=(n // FFN_TM,),
        in_specs=in_specs,
        out_specs=row,
        out_shape=jax.ShapeDtypeStruct((n, d), F32),
        compiler_params=_params(("parallel",)),
        name="ffn",
    )(*args)


PROJ_TM = 512


def _rope_table_kernel(pos_ref, inv_ref, c_ref, sa_ref, sb_ref):
    ang = pos_ref[...].astype(F32) * inv_ref[...]
    lane = lax.broadcasted_iota(jnp.int32, ang.shape, 1) % DIFF_HEAD_DIM
    c = jnp.cos(ang)
    s = jnp.sin(ang)
    half = ROT_DIM // 2
    c_ref[...] = jnp.where(lane < ROT_DIM, c, 1.0)
    sa_ref[...] = jnp.where(lane < half, -s, 0.0)
    sb_ref[...] = jnp.where((lane >= half) & (lane < ROT_DIM), s, 0.0)


def _rope_tables(positions):
    n = positions.size
    half = ROT_DIM // 2
    inv = ROPE_THETA ** (-jnp.arange(0, ROT_DIM, 2, dtype=F32) / ROT_DIM)
    inv_lane = jnp.tile(inv, LANES // half).reshape(1, LANES)
    tm = 1024
    out = pl.BlockSpec((tm, LANES), lambda i: (i, 0))
    return pl.pallas_call(
        _rope_table_kernel,
        grid=(n // tm,),
        in_specs=[pl.BlockSpec((tm, 1), lambda i: (i, 0)), _resident((1, LANES))],
        out_specs=[out, out, out],
        out_shape=[jax.ShapeDtypeStruct((n, LANES), F32)] * 3,
        compiler_params=_params(("parallel",)),
        name="rope_tables",
    )(positions.reshape(n, 1), inv_lane)


def _norm_proj_kernel(x_ref, g_ref, w_ref, *rest, rope_cols, rope_scale, nat_cols, t_cols, sm_cols):
    h = _rms(x_ref[...], g_ref[...]).astype(BF16)
    y = jnp.dot(h, w_ref[...], preferred_element_type=F32)
    blocks = [y[:, t * LANES:(t + 1) * LANES] for t in range(y.shape[1] // LANES)]
    rest = list(rest)
    if rope_cols:
        c, sa, sb = (r[...] for r in rest[:3])
        rest = rest[3:]
        half = ROT_DIM // 2
        for t in range(rope_cols // LANES):
            blk = blocks[t]
            r = blk * c + pltpu.roll(blk, LANES - half, 1) * sa + pltpu.roll(blk, half, 1) * sb
            blocks[t] = r * rope_scale if rope_scale != 1.0 else r
    if nat_cols:
        o_ref = rest.pop(0)
        lo, hi = nat_cols
        for t in range(lo // LANES, hi // LANES):
            o_ref[:, t * LANES - lo:(t + 1) * LANES - lo] = blocks[t].astype(o_ref.dtype)
    if t_cols:
        ot_ref = rest.pop(0)
        lo, hi = t_cols
        for t in range(lo // LANES, hi // LANES):
            ot_ref[t * LANES - lo:(t + 1) * LANES - lo, :] = blocks[t].T.astype(ot_ref.dtype)
    if sm_cols:
        os_ref, sm_sc = rest
        lo, hi = sm_cols
        for t in range(lo // LANES, hi // LANES):
            sm_sc[t - lo // LANES] = blocks[t]
            for s in range(S5_CHUNK):
                os_ref[s, :, t * LANES - lo:(t + 1) * LANES - lo] = sm_sc[
                    t - lo // LANES, pl.ds(s, os_ref.shape[1], stride=S5_CHUNK), :].astype(os_ref.dtype)


def _norm_proj(x, g, w, rope=None, rope_cols=0, rope_scale=1.0, nat_cols=None, t_cols=None,
               sm_cols=None, tm=PROJ_TM):
    n, d = x.shape
    nout = w.shape[1]
    tm = min(tm, n)
    if nat_cols is None and t_cols is None and sm_cols is None:
        nat_cols = (0, nout)
    in_specs = [pl.BlockSpec((tm, d), lambda i: (i, 0)), _resident((1, d)), _resident(w.shape)]
    args = [x, g.reshape(1, d), w]
    if rope_cols:
        tab = pl.BlockSpec((tm, LANES), lambda i: (i, 0))
        in_specs += [tab, tab, tab]
        args += list(rope)
    out_specs, out_shape = [], []
    if nat_cols:
        width = nat_cols[1] - nat_cols[0]
        out_specs.append(pl.BlockSpec((tm, width), lambda i: (i, 0)))
        out_shape.append(jax.ShapeDtypeStruct((n, width), BF16))
    if t_cols:
        width = t_cols[1] - t_cols[0]
        out_specs.append(pl.BlockSpec((None, width, tm), lambda i: (i, 0, 0)))
        out_shape.append(jax.ShapeDtypeStruct((n // tm, width, tm), BF16))
    scratch = []
    if sm_cols:
        width = sm_cols[1] - sm_cols[0]
        out_specs.append(pl.BlockSpec((S5_CHUNK, tm // S5_CHUNK, width), lambda i: (0, i, 0)))
        out_shape.append(jax.ShapeDtypeStruct((S5_CHUNK, n // S5_CHUNK, width), BF16))
        scratch.append(pltpu.VMEM((width // LANES, tm, LANES), F32))
    out = pl.pallas_call(
        functools.partial(_norm_proj_kernel, rope_cols=rope_cols, rope_scale=rope_scale,
                          nat_cols=nat_cols, t_cols=t_cols, sm_cols=sm_cols),
        grid=(n // tm,),
        in_specs=in_specs,
        out_specs=out_specs,
        out_shape=out_shape,
        scratch_shapes=scratch,
        compiler_params=_params(("parallel",)),
        name="norm_proj",
    )(*args)
    return out[0] if len(out) == 1 else out


S5_POW_ROWS = 24


def _s5_prep_kernel(are_ref, aim_ref, ldt_ref, btr_ref, bti_ref, cr_ref, ci_ref, dd_ref,
                    t_ref, wend_ref, et_ref, da_ref, db_ref, e_sc, *, n_levels):
    lr = are_ref[...]
    li = aim_ref[...]
    dt = jnp.exp(ldt_ref[...])
    lane = lax.broadcasted_iota(jnp.int32, (1, LANES), 1)
    lo = lane < S5_STATE

    def powers(k):
        mag = jnp.exp(k * (lr * dt))
        ang = k * (li * dt)
        return mag * jnp.cos(ang), mag * jnp.sin(ang)

    kf = lax.broadcasted_iota(jnp.int32, (S5_POW_ROWS, 1), 0).astype(F32)
    pr, pi = powers(kf)
    p_ri = jnp.where(lo, pr, pi)
    p_mir = jnp.where(lo, -pi, pr)
    p_rmi = jnp.where(lo, pr, -pi)
    p_mimr = jnp.where(lo, -pi, -pr)

    abr, abi = pr[1:2], pi[1:2]
    den = lr * lr + li * li
    nr, ni = abr - 1.0, abi
    fr = (nr * lr + ni * li) / den
    fi = (ni * lr - nr * li) / den
    btr, bti = btr_ref[...], bti_ref[...]
    bbr = fr * btr - fi * bti
    bbi = fr * bti + fi * btr

    cr, ci = cr_ref[...], ci_ref[...]
    for k in range(S5_CHUNK + 1):
        e_sc[k * S5_GROUP:(k + 1) * S5_GROUP, :] = cr * p_rmi[k:k + 1] + ci * p_mimr[k:k + 1]
    et_ref[...] = e_sc[S5_GROUP:, :].astype(et_ref.dtype)

    for s in range(S5_CHUNK):
        k = S5_CHUNK - 1 - s
        w = bbr * p_ri[k:k + 1] + bbi * p_mir[k:k + 1]
        wend_ref[s * S5_GROUP:(s + 1) * S5_GROUP, :] = w.astype(wend_ref.dtype)

    bb = jnp.where(lo, bbr, bbi)
    kern = lax.dot_general(bb, e_sc[:S5_FLAT, :], _NT, precision=lax.Precision.HIGHEST,
                           preferred_element_type=F32)
    col = lax.broadcasted_iota(jnp.int32, (S5_GROUP, S5_FLAT), 1)
    row = lax.broadcasted_iota(jnp.int32, (S5_GROUP, S5_FLAT), 0)
    dd = dd_ref[...]
    for s in range(S5_CHUNK):
        shifted = kern if s == 0 else pltpu.roll(kern, s * S5_GROUP, 1)
        blk = jnp.where(col >= s * S5_GROUP, shifted, 0.0)
        blk = blk + jnp.where(col == row + s * S5_GROUP, dd, 0.0)
        t_ref[s * S5_GROUP:(s + 1) * S5_GROUP, :] = blk.astype(t_ref.dtype)

    steps = (S5_CHUNK * 2.0 ** lax.broadcasted_iota(jnp.int32, (n_levels, 1), 0).astype(F32))
    qr, qi = powers(steps)
    da_ref[...] = qr
    db_ref[...] = jnp.where(lo, -qi, qi)


def _dup(a):
    return jnp.concatenate([a, a], axis=-1)


def _s5_prep(a_re, a_im, log_dt, b_re, b_im, c_re, c_im, d, n_levels):
    g = S5_GROUPS
    args = [
        _dup(a_re).reshape(g, 1, LANES), _dup(a_im).reshape(g, 1, LANES), log_dt.reshape(g, 1, 1),
        _dup(jnp.swapaxes(b_re, 1, 2)), _dup(jnp.swapaxes(b_im, 1, 2)), _dup(c_re), _dup(c_im),
        jnp.tile(d, (1, S5_CHUNK)).reshape(g, 1, S5_FLAT),
    ]

    def per_group(*shape):
        return pl.BlockSpec((None,) + shape, lambda i: (i,) + (0,) * len(shape))

    in_specs = [per_group(1, LANES), per_group(1, LANES), per_group(1, 1),
                per_group(S5_GROUP, LANES), per_group(S5_GROUP, LANES),
                per_group(S5_GROUP, LANES), per_group(S5_GROUP, LANES), per_group(1, S5_FLAT)]
    out_specs = [per_group(S5_FLAT, S5_FLAT), per_group(S5_FLAT, LANES), per_group(S5_FLAT, LANES),
                 per_group(n_levels, LANES), per_group(n_levels, LANES)]
    out_shape = [jax.ShapeDtypeStruct((g, S5_FLAT, S5_FLAT), BF16),
                 jax.ShapeDtypeStruct((g, S5_FLAT, LANES), BF16),
                 jax.ShapeDtypeStruct((g, S5_FLAT, LANES), BF16),
                 jax.ShapeDtypeStruct((g, n_levels, LANES), F32),
                 jax.ShapeDtypeStruct((g, n_levels, LANES), F32)]
    return pl.pallas_call(
        functools.partial(_s5_prep_kernel, n_levels=n_levels),
        grid=(g,),
        in_specs=in_specs,
        out_specs=out_specs,
        out_shape=out_shape,
        scratch_shapes=[pltpu.VMEM(((S5_CHUNK + 1) * S5_GROUP, LANES), F32)],
        compiler_params=_params(("parallel",)),
        name="s5_prep",
    )(*args)


S5_GPB = LANES // S5_GROUP


def _s5_main_kernel(us_ref, t_ref, wend_ref, et_ref, da_ref, db_ref, ys_ref, y_sc, *, nb, n_levels):
    r = us_ref.shape[1] // nb
    rows = lax.broadcasted_iota(jnp.int32, (r, LANES), 0)
    piece = lax.broadcasted_iota(jnp.int32, (r, LANES), 1) // S5_GROUP

    def swap_pieces(arrs):
        arrs = list(arrs)
        d = S5_GPB // 2
        while d:
            hi = (piece & d) != 0
            up, down = S5_GROUP * d, LANES - S5_GROUP * d
            for i in range(S5_GPB):
                if i & d:
                    continue
                a, b = arrs[i], arrs[i + d]
                arrs[i] = jnp.where(hi, pltpu.roll(b, up, 1), a)
                arrs[i + d] = jnp.where(hi, b, pltpu.roll(a, down, 1))
            d //= 2
        return arrs

    def per_batch(b, carry):
        row0 = pl.multiple_of(b * r, r)
        xs = [us_ref[s, pl.ds(row0, r), :].astype(F32) for s in range(S5_CHUNK)]
        folded = [swap_pieces(xs[S5_GPB * hf:S5_GPB * (hf + 1)]) for hf in range(2)]
        for gl in range(S5_GPB):
            u = jnp.concatenate([folded[0][gl], folded[1][gl]], axis=1).astype(BF16)
            y = jnp.dot(u, t_ref[gl], preferred_element_type=F32)
            x = jnp.dot(u, wend_ref[gl], preferred_element_type=F32)
            da, db = da_ref[gl], db_ref[gl]
            for m in range(n_levels):
                d = 1 << m
                sh = jnp.where(rows >= d, pltpu.roll(x, d, 0), 0.0)
                x = x + sh * da[m:m + 1] + pltpu.roll(sh, S5_STATE, 1) * db[m:m + 1]
            hp = jnp.where(rows >= 1, pltpu.roll(x, 1, 0), 0.0).astype(BF16)
            y_sc[gl] = y + lax.dot_general(hp, et_ref[gl], _NT, preferred_element_type=F32)
        for hf in range(2):
            steps = swap_pieces([y_sc[gl, :, hf * LANES:(hf + 1) * LANES] for gl in range(S5_GPB)])
            for tp in range(S5_GPB):
                ys_ref[S5_GPB * hf + tp, pl.ds(row0, r), :] = steps[tp].astype(ys_ref.dtype)
        return carry

    lax.fori_loop(0, nb, per_batch, 0)


def _s5_main(us, ops, nb, n_levels):
    _, rows, width = us.shape
    t, wend, et, da, db = ops

    def per_block(*shape):
        return pl.BlockSpec((S5_GPB,) + shape, lambda i: (i,) + (0,) * len(shape))

    io = pl.BlockSpec((S5_CHUNK, rows, LANES), lambda i: (0, 0, i))
    return pl.pallas_call(
        functools.partial(_s5_main_kernel, nb=nb, n_levels=n_levels),
        grid=(width // LANES,),
        in_specs=[io, per_block(S5_FLAT, S5_FLAT), per_block(S5_FLAT, LANES),
                  per_block(S5_FLAT, LANES), per_block(n_levels, LANES), per_block(n_levels, LANES)],
        out_specs=io,
        out_shape=jax.ShapeDtypeStruct(us.shape, BF16),
        scratch_shapes=[pltpu.VMEM((S5_GPB, rows // nb, S5_FLAT), F32)],
        compiler_params=_params(("parallel",)),
        name="s5_main",
    )(us, t, wend, et, da, db)


S5_POST_TM = 1024


def _s5_post_kernel(y_ref, w_ref, o_ref):
    z = jax.nn.gelu(y_ref[...].astype(F32), approximate=True)
    gate = jnp.dot(z.astype(BF16), w_ref[...], preferred_element_type=F32)
    o_ref[...] = (z * jax.nn.sigmoid(gate)).astype(o_ref.dtype)


def _s5_post(y, w_glu):
    n, w = y.shape
    tm = min(S5_POST_TM, n)
    row = pl.BlockSpec((tm, w), lambda i: (i, 0))
    return pl.pallas_call(
        _s5_post_kernel,
        grid=(n // tm,),
        in_specs=[row, _resident(w_glu.shape)],
        out_specs=row,
        out_shape=jax.ShapeDtypeStruct((n, w), BF16),
        compiler_params=_params(("parallel",)),
        name="s5_post",
    )(y, w_glu)


def _s5_mixer(us, bsz, seq, prep_args, w_glu):
    r = seq // S5_CHUNK
    n_levels = int(math.log2(r))
    assert 1 << n_levels == r
    ops = _s5_prep(*prep_args, n_levels=n_levels)
    ys = _s5_main(us, ops, bsz, n_levels)
    return _s5_post(ys.reshape(bsz * seq, TOK_WIDTH), w_glu).reshape(us.shape)


ATTN_T = 256
QK_SCALE = DIFF_HEAD_DIM ** -0.5 * math.log2(math.e)


ATTN_HB = 6
ATTN_SUM_ROWS = 16


def _diff_attn_kernel(qt_ref, k_ref, vt_ref, lq1_ref, lk1_ref, lq2_ref, lk2_ref, g_ref, o_ref,
                      *, lam_init):
    t = ATTN_T
    hw = 2 * DIFF_HEAD_DIM
    i = pl.program_id(2)
    qz = []
    for h in range(ATTN_HB):
        qt = qt_ref[h * hw:(h + 1) * hw, :].astype(F32)
        sub = lax.broadcasted_iota(jnp.int32, qt.shape, 0) // DIFF_HEAD_DIM
        qz.append(jnp.concatenate([jnp.where(sub == c, qt, 0.0) for c in range(2)],
                                  axis=1).astype(BF16))

    def step(j, carry, mask):
        kb = k_ref[pl.ds(pl.multiple_of(j * t, t), t), :]
        vtb = vt_ref[j]
        ss = [jnp.dot(kb[:, h * hw:(h + 1) * hw], qz[h], preferred_element_type=F32)
              for h in range(ATTN_HB)]
        ones = jnp.ones((ATTN_SUM_ROWS, t), BF16)
        out = []
        for h in range(ATTN_HB):
            m, acc = carry[h]
            s = ss[h] if mask is None else jnp.where(mask, ss[h], -jnp.inf)
            m_new = jnp.maximum(m, jnp.max(s, axis=0, keepdims=True))
            alpha = jnp.exp2(m - m_new)
            p = jnp.exp2(s - m_new).astype(BF16)
            v1 = jnp.concatenate([vtb[h * DIFF_V_DIM:(h + 1) * DIFF_V_DIM, :], ones], axis=0)
            pv = jnp.dot(v1, p, preferred_element_type=F32)
            out.append((m_new, alpha * acc + pv))
        return tuple(out)

    init = tuple((jnp.full((1, 2 * t), -1e30, F32),
                  jnp.zeros((DIFF_V_DIM + ATTN_SUM_ROWS, 2 * t), F32)) for _ in range(ATTN_HB))
    carry = lax.fori_loop(0, i, lambda j, c: step(j, c, None), init)
    key = lax.broadcasted_iota(jnp.int32, (t, 2 * t), 0)
    qry = lax.broadcasted_iota(jnp.int32, (t, 2 * t), 1) % t
    carry = step(i, carry, key <= qry)

    lam = (jnp.exp(jnp.sum(lq1_ref[...] * lk1_ref[...], axis=-1, keepdims=True))
           - jnp.exp(jnp.sum(lq2_ref[...] * lk2_ref[...], axis=-1, keepdims=True)) + lam_init)
    for h in range(ATTN_HB):
        _, acc = carry[h]
        r = acc[:DIFF_V_DIM] / acc[DIFF_V_DIM:DIFF_V_DIM + 1]
        o = r[:, :t] - lam * r[:, t:]
        o = o * lax.rsqrt(jnp.mean(o * o, axis=0, keepdims=True) + EPS) * g_ref[...] * (1.0 - lam_init)
        o_ref[:, h * DIFF_V_DIM:(h + 1) * DIFF_V_DIM] = o.T.astype(o_ref.dtype)


def _diff_attn(qt, k, vt, bsz, seq, lq1, lk1, lq2, lk2, subln, lam_init):
    t = ATTN_T
    n = k.shape[0]
    nq = seq // t
    hw = ATTN_HB * DIFF_V_DIM
    vec = _resident((1, DIFF_HEAD_DIM))
    return pl.pallas_call(
        functools.partial(_diff_attn_kernel, lam_init=lam_init),
        grid=(bsz, DIFF_HEADS // ATTN_HB, nq),
        in_specs=[
            pl.BlockSpec((None, hw, t), lambda b, h, i: (b * nq + i, h, 0)),
            pl.BlockSpec((seq, hw), lambda b, h, i: (b, h)),
            pl.BlockSpec((nq, hw, t), lambda b, h, i: (b, h, 0)),
            vec, vec, vec, vec, _resident((DIFF_V_DIM, 1)),
        ],
        out_specs=pl.BlockSpec((t, hw), lambda b, h, i: (b * nq + i, h)),
        out_shape=jax.ShapeDtypeStruct((n, TOK_WIDTH), BF16),
        compiler_params=_params(("parallel", "parallel", "arbitrary")),
        name="diff_attn",
    )(qt, k, vt, lq1.reshape(1, -1), lk1.reshape(1, -1), lq2.reshape(1, -1), lk2.reshape(1, -1),
      subln.reshape(-1, 1))


MIX_TM = 512


MEM_SUM_ROWS = 16


def _mix_out_kernel(x_ref, tok_ref, mqt_ref, mk_ref, mvt_ref, w_ref, o_ref, *scratch):
    if scratch:
        (tok_sc,) = scratch
        for t in range(TOK_WIDTH // LANES):
            for s in range(S5_CHUNK):
                tok_sc[t, pl.ds(s, tok_ref.shape[1], stride=S5_CHUNK), :] = (
                    tok_ref[s, :, t * LANES:(t + 1) * LANES].astype(F32))
        tok = jnp.concatenate([tok_sc[t] for t in range(TOK_WIDTH // LANES)], axis=1).astype(BF16)
    else:
        tok = tok_ref[...]
    acc = x_ref[...] + jnp.dot(tok, w_ref[:TOK_WIDTH, :], preferred_element_type=F32)

    mqt = jnp.concatenate([mqt_ref[j] for j in range(mqt_ref.shape[0])], axis=1)
    mqt = mqt.astype(F32) * MEM_HEAD_DIM ** -0.5
    head = lax.broadcasted_iota(jnp.int32, mqt.shape, 0) // MEM_HEAD_DIM
    mk = mk_ref[...]
    mvt = mvt_ref[...]
    ones = jnp.ones((MEM_SUM_ROWS, mvt.shape[1]), BF16)
    heads = []
    for h in range(MEM_HEADS):
        qz = jnp.where(head == h, mqt, 0.0).astype(BF16)
        s = jnp.dot(mk, qz, preferred_element_type=F32)
        p = jnp.exp(s - jnp.max(s, axis=0, keepdims=True)).astype(BF16)
        v1 = jnp.concatenate([mvt[h * MEM_HEAD_DIM:(h + 1) * MEM_HEAD_DIM, :], ones], axis=0)
        pv = jnp.dot(v1, p, preferred_element_type=F32)
        heads.append(pv[:MEM_HEAD_DIM] / pv[MEM_HEAD_DIM:MEM_HEAD_DIM + 1])
    mo = jnp.concatenate(heads, axis=0).T.astype(BF16)
    o_ref[...] = acc + jnp.dot(mo, w_ref[TOK_WIDTH:, :], preferred_element_type=F32)


def _mix_out(x, tok, mqt, mk, mvt, w, bsz, seq):
    n, d = x.shape
    tm = min(MIX_TM, seq)
    ns = seq // tm
    tq = mqt.shape[2]
    mq_block = mqt.shape[1] // MEM_WIDTH - 1
    row = pl.BlockSpec((tm, d), lambda b, i: (b * ns + i, 0))
    if tok.ndim == 3:
        tok_spec = pl.BlockSpec((S5_CHUNK, tm // S5_CHUNK, TOK_WIDTH), lambda b, i: (0, b * ns + i, 0))
        scratch = [pltpu.VMEM((TOK_WIDTH // LANES, tm, LANES), F32)]
    else:
        tok_spec = pl.BlockSpec((tm, TOK_WIDTH), lambda b, i: (b * ns + i, 0))
        scratch = []
    return pl.pallas_call(
        _mix_out_kernel,
        grid=(bsz, ns),
        in_specs=[
            row,
            tok_spec,
            pl.BlockSpec((tm // tq, MEM_WIDTH, tq), lambda b, i: (b * ns + i, mq_block, 0)),
            pl.BlockSpec((N_MEM, MEM_WIDTH), lambda b, i: (b, 0)),
            pl.BlockSpec((None, MEM_WIDTH, N_MEM), lambda b, i: (b, 0, 0)),
            _resident(w.shape),
        ],
        out_specs=row,
        out_shape=jax.ShapeDtypeStruct((n, d), F32),
        scratch_shapes=scratch,
        compiler_params=_params(("parallel", "parallel")),
        name="mix_out",
    )(x, tok, mqt, mk, mvt, w)


def kernel(x, mem, positions, ln_ffn1, ffn1_in, ffn1_out, ln_mix, w_mix_in, w_mix_out, ln_mem, w_mem_kv, ln_ffn2, ffn2_in, ffn2_out, s5_a_re, s5_a_im, s5_log_dt, s5_b_re, s5_b_im, s5_c_re, s5_c_im, s5_d, s5_w_glu, ln_kv, w_kv_shared, diff_lq1, diff_lk1, diff_lq2, diff_lk2, diff_subln, ln_final):
    bsz, seq, d = x.shape
    n = bsz * seq
    xf = x.reshape(n, d)
    memf = mem.reshape(bsz * N_MEM, d)
    rope = _rope_tables(positions)
    k_sh = vt_sh = None
    for i in range(DEPTH):
        if i == N_A:
            k_sh, vt_sh = _norm_proj(xf, ln_kv, w_kv_shared.astype(BF16), rope, QK_WIDTH,
                                     nat_cols=(0, QK_WIDTH), t_cols=(QK_WIDTH, 2 * QK_WIDTH), tm=ATTN_T)
        xf = _ffn(xf, ln_ffn1[i], ffn1_in[i].astype(BF16), ffn1_out[i].astype(BF16))
        if i < N_A:
            mqt, us = _norm_proj(xf, ln_mix[i], w_mix_in[i].astype(BF16),
                                 t_cols=(TOK_WIDTH, D_MODEL), sm_cols=(0, TOK_WIDTH))
            tok = _s5_mixer(us, bsz, seq,
                            (s5_a_re[i], s5_a_im[i], s5_log_dt[i], s5_b_re[i], s5_b_im[i],
                             s5_c_re[i], s5_c_im[i], s5_d[i]), s5_w_glu[i].astype(BF16))
        else:
            j = i - N_A
            mqt = _norm_proj(xf, ln_mix[i], w_mix_in[i].astype(BF16), rope, TOK_WIDTH, QK_SCALE,
                             t_cols=(0, D_MODEL), tm=ATTN_T)
            lam_init = 0.8 - 0.6 * math.exp(-0.3 * i)
            tok = _diff_attn(mqt, k_sh, vt_sh, bsz, seq, diff_lq1[j], diff_lk1[j], diff_lq2[j],
                             diff_lk2[j], diff_subln[j], lam_init)
        mk, mvt = _norm_proj(memf, ln_mem, w_mem_kv[i].astype(BF16),
                             nat_cols=(0, MEM_WIDTH), t_cols=(MEM_WIDTH, 2 * MEM_WIDTH), tm=N_MEM)
        xf = _mix_out(xf, tok, mqt, mk, mvt, w_mix_out[i].astype(BF16), bsz, seq)
        xf = _ffn(xf, ln_ffn2[i], ffn2_in[i].astype(BF16), ffn2_out[i].astype(BF16),
                  final_g=ln_final if i == DEPTH - 1 else None)
    return xf.reshape(bsz, seq, d)
```

```python
import functools
import math

import jax
import jax.numpy as jnp
from jax import lax
from jax.experimental import pallas as pl
from jax.experimental.pallas import tpu as pltpu

F32 = jnp.float32
BF16 = jnp.bfloat16

D_MODEL = 1024
DEPTH = 4
N_A = DEPTH // 2
TOK_WIDTH = 768
MEM_HEADS = 4
MEM_HEAD_DIM = 64
MEM_WIDTH = MEM_HEADS * MEM_HEAD_DIM
N_MEM = 256
S5_GROUP = 16
S5_GROUPS = TOK_WIDTH // S5_GROUP
S5_STATE = 64
S5_CHUNK = 16
S5_FLAT = S5_CHUNK * S5_GROUP
DIFF_HEAD_DIM = 64
DIFF_HEADS = 6
DIFF_V_DIM = 128
QK_WIDTH = 768
ROT_DIM = 16
ROPE_THETA = 500000.0
D_FF = 2816
EPS = 1e-6

LANES = 128
VMEM_LIMIT = 56 * 1024 * 1024

_NT = (((1,), (1,)), ((), ()))


def _params(sem, vmem=VMEM_LIMIT):
    return pltpu.CompilerParams(dimension_semantics=sem, vmem_limit_bytes=vmem)


def _resident(shape):
    nd = len(shape)
    return pl.BlockSpec(shape, lambda *_: (0,) * nd, pipeline_mode=pl.Buffered(1))


def _rms(x, g):
    return x * lax.rsqrt(jnp.mean(x * x, axis=-1, keepdims=True) + EPS) * g


FFN_TM = 512
FFN_TF = 256


def _ffn_kernel(x_ref, g_ref, win_ref, wout_ref, *rest, final):
    if final:
        gf_ref, o_ref = rest
    else:
        (o_ref,) = rest
    x = x_ref[...]
    h = _rms(x, g_ref[...]).astype(BF16)
    acc = x
    for c in range(D_FF // FFN_TF):
        lo = c * FFN_TF
        gate = jnp.dot(h, win_ref[:, lo:lo + FFN_TF].astype(BF16), preferred_element_type=F32)
        up = jnp.dot(h, win_ref[:, D_FF + lo:D_FF + lo + FFN_TF].astype(BF16),
                     preferred_element_type=F32)
        a = (gate * jax.nn.sigmoid(gate) * (0.5 * up)).astype(BF16)
        acc = acc + jnp.dot(a, wout_ref[lo:lo + FFN_TF, :].astype(BF16), preferred_element_type=F32)
    if final:
        acc = _rms(acc, gf_ref[...])
    o_ref[...] = acc


def _ffn(x, g, w_in, w_out, final_g=None):
    n, d = x.shape
    row = pl.BlockSpec((FFN_TM, d), lambda i: (i, 0))
    in_specs = [row, _resident((1, d)), _resident(w_in.shape), _resident(w_out.shape)]
    args = [x, g.reshape(1, d), w_in, w_out]
    if final_g is not None:
        in_specs.append(_resident((1, d)))
        args.append(final_g.reshape(1, d))
    return pl.pallas_call(
        functools.partial(_ffn_kernel, final=final_g is not None),
        grid=(n // FFN_TM,),
        in_specs=in_specs,
        out_specs=row,
        out_shape=jax.ShapeDtypeStruct((n, d), F32),
        compiler_params=_params(("parallel",)),
        name="ffn",
    )(*args)


PROJ_TM = 512


def _rope_table_kernel(pos_ref, inv_ref, c_ref, sa_ref, sb_ref):
    ang = pos_ref[...].astype(F32) * inv_ref[...]
    lane = lax.broadcasted_iota(jnp.int32, ang.shape, 1) % DIFF_HEAD_DIM
    c = jnp.cos(ang)
    s = jnp.sin(ang)
    half = ROT_DIM // 2
    c_ref[...] = jnp.where(lane < ROT_DIM, c, 1.0)
    sa_ref[...] = jnp.where(lane < half, -s, 0.0)
    sb_ref[...] = jnp.where((lane >= half) & (lane < ROT_DIM), s, 0.0)


def _rope_tables(positions):
    n = positions.size
    half = ROT_DIM // 2
    inv = ROPE_THETA ** (-jnp.arange(0, ROT_DIM, 2, dtype=F32) / ROT_DIM)
    inv_lane = jnp.tile(inv, LANES // half).reshape(1, LANES)
    tm = 1024
    out = pl.BlockSpec((tm, LANES), lambda i: (i, 0))
    return pl.pallas_call(
        _rope_table_kernel,
        grid=(n // tm,),
        in_specs=[pl.BlockSpec((tm, 1), lambda i: (i, 0)), _resident((1, LANES))],
        out_specs=[out, out, out],
        out_shape=[jax.ShapeDtypeStruct((n, LANES), F32)] * 3,
        compiler_params=_params(("parallel",)),
        name="rope_tables",
    )(positions.reshape(n, 1), inv_lane)


def _norm_proj_kernel(x_ref, g_ref, w_ref, *rest, rope_cols, rope_scale, nat_cols, t_cols, sm_cols):
    h = _rms(x_ref[...], g_ref[...]).astype(BF16)
    y = jnp.dot(h, w_ref[...], preferred_element_type=F32)
    blocks = [y[:, t * LANES:(t + 1) * LANES] for t in range(y.shape[1] // LANES)]
    rest = list(rest)
    if rope_cols:
        c, sa, sb = (r[...] for r in rest[:3])
        rest = rest[3:]
        half = ROT_DIM // 2
        for t in range(rope_cols // LANES):
            blk = blocks[t]
            r = blk * c + pltpu.roll(blk, LANES - half, 1) * sa + pltpu.roll(blk, half, 1) * sb
            blocks[t] = r * rope_scale if rope_scale != 1.0 else r
    if nat_cols:
        o_ref = rest.pop(0)
        lo, hi = nat_cols
        for t in range(lo // LANES, hi // LANES):
            o_ref[:, t * LANES - lo:(t + 1) * LANES - lo] = blocks[t].astype(o_ref.dtype)
    if t_cols:
        ot_ref = rest.pop(0)
        lo, hi = t_cols
        for t in range(lo // LANES, hi // LANES):
            for j in range(ot_ref.shape[0]):
                tw = ot_ref.shape[2]
                ot_ref[j, t * LANES - lo:(t + 1) * LANES - lo, :] = (
                    blocks[t][j * tw:(j + 1) * tw, :].T.astype(ot_ref.dtype))
    if sm_cols:
        os_ref, sm_sc = rest
        lo, hi = sm_cols
        for t in range(lo // LANES, hi // LANES):
            sm_sc[t - lo // LANES] = blocks[t]
            for s in range(S5_CHUNK):
                os_ref[s, :, t * LANES - lo:(t + 1) * LANES - lo] = sm_sc[
                    t - lo // LANES, pl.ds(s, os_ref.shape[1], stride=S5_CHUNK), :].astype(os_ref.dtype)


def _norm_proj(x, g, w, rope=None, rope_cols=0, rope_scale=1.0, nat_cols=None, t_cols=None,
               sm_cols=None, tm=PROJ_TM, t_tile=None):
    n, d = x.shape
    nout = w.shape[1]
    tm = min(tm, n)
    if nat_cols is None and t_cols is None and sm_cols is None:
        nat_cols = (0, nout)
    in_specs = [pl.BlockSpec((tm, d), lambda i: (i, 0)), _resident((1, d)), _resident(w.shape)]
    args = [x, g.reshape(1, d), w]
    if rope_cols:
        tab = pl.BlockSpec((tm, LANES), lambda i: (i, 0))
        in_specs += [tab, tab, tab]
        args += list(rope)
    out_specs, out_shape = [], []
    if nat_cols:
        width = nat_cols[1] - nat_cols[0]
        out_specs.append(pl.BlockSpec((tm, width), lambda i: (i, 0)))
        out_shape.append(jax.ShapeDtypeStruct((n, width), BF16))
    if t_cols:
        width = t_cols[1] - t_cols[0]
        tw = min(t_tile or tm, tm)
        out_specs.append(pl.BlockSpec((tm // tw, width, tw), lambda i: (i, 0, 0)))
        out_shape.append(jax.ShapeDtypeStruct((n // tw, width, tw), BF16))
    scratch = []
    if sm_cols:
        width = sm_cols[1] - sm_cols[0]
        out_specs.append(pl.BlockSpec((S5_CHUNK, tm // S5_CHUNK, width), lambda i: (0, i, 0)))
        out_shape.append(jax.ShapeDtypeStruct((S5_CHUNK, n // S5_CHUNK, width), BF16))
        scratch.append(pltpu.VMEM((width // LANES, tm, LANES), F32))
    out = pl.pallas_call(
        functools.partial(_norm_proj_kernel, rope_cols=rope_cols, rope_scale=rope_scale,
                          nat_cols=nat_cols, t_cols=t_cols, sm_cols=sm_cols),
        grid=(n // tm,),
        in_specs=in_specs,
        out_specs=out_specs,
        out_shape=out_shape,
        scratch_shapes=scratch,
        compiler_params=_params(("parallel",)),
        name="norm_proj",
    )(*args)
    return out[0] if len(out) == 1 else out


S5_POW_ROWS = 24


def _s5_prep_kernel(are_ref, aim_ref, ldt_ref, btr_ref, bti_ref, cr_ref, ci_ref, dd_ref,
                    t_ref, wend_ref, et_ref, da_ref, db_ref, e_sc, *, n_levels):
    lr = are_ref[...]
    li = aim_ref[...]
    dt = jnp.exp(ldt_ref[...])
    lane = lax.broadcasted_iota(jnp.int32, (1, LANES), 1)
    lo = lane < S5_STATE

    def powers(k):
        mag = jnp.exp(k * (lr * dt))
        ang = k * (li * dt)
        return mag * jnp.cos(ang), mag * jnp.sin(ang)

    kf = lax.broadcasted_iota(jnp.int32, (S5_POW_ROWS, 1), 0).astype(F32)
    pr, pi = powers(kf)
    p_ri = jnp.where(lo, pr, pi)
    p_mir = jnp.where(lo, -pi, pr)
    p_rmi = jnp.where(lo, pr, -pi)
    p_mimr = jnp.where(lo, -pi, -pr)

    abr, abi = pr[1:2], pi[1:2]
    den = lr * lr + li * li
    nr, ni = abr - 1.0, abi
    fr = (nr * lr + ni * li) / den
    fi = (ni * lr - nr * li) / den
    btr, bti = btr_ref[...], bti_ref[...]
    bbr = fr * btr - fi * bti
    bbi = fr * bti + fi * btr

    cr, ci = cr_ref[...], ci_ref[...]
    for k in range(S5_CHUNK + 1):
        e_sc[k * S5_GROUP:(k + 1) * S5_GROUP, :] = cr * p_rmi[k:k + 1] + ci * p_mimr[k:k + 1]
    et_ref[...] = e_sc[S5_GROUP:, :].astype(et_ref.dtype)

    for s in range(S5_CHUNK):
        k = S5_CHUNK - 1 - s
        w = bbr * p_ri[k:k + 1] + bbi * p_mir[k:k + 1]
        wend_ref[s * S5_GROUP:(s + 1) * S5_GROUP, :] = w.astype(wend_ref.dtype)

    bb = jnp.where(lo, bbr, bbi)
    kern = lax.dot_general(bb, e_sc[:S5_FLAT, :], _NT, precision=lax.Precision.HIGHEST,
                           preferred_element_type=F32)
    col = lax.broadcasted_iota(jnp.int32, (S5_GROUP, S5_FLAT), 1)
    row = lax.broadcasted_iota(jnp.int32, (S5_GROUP, S5_FLAT), 0)
    dd = dd_ref[...]
    for s in range(S5_CHUNK):
        shifted = kern if s == 0 else pltpu.roll(kern, s * S5_GROUP, 1)
        blk = jnp.where(col >= s * S5_GROUP, shifted, 0.0)
        blk = blk + jnp.where(col == row + s * S5_GROUP, dd, 0.0)
        t_ref[s * S5_GROUP:(s + 1) * S5_GROUP, :] = blk.astype(t_ref.dtype)

    steps = (S5_CHUNK * 2.0 ** lax.broadcasted_iota(jnp.int32, (n_levels, 1), 0).astype(F32))
    qr, qi = powers(steps)
    da_ref[...] = qr
    db_ref[...] = jnp.where(lo, -qi, qi)


def _dup(a):
    return jnp.concatenate([a, a], axis=-1)


def _s5_prep(a_re, a_im, log_dt, b_re, b_im, c_re, c_im, d, n_levels):
    g = S5_GROUPS
    args = [
        _dup(a_re).reshape(g, 1, LANES), _dup(a_im).reshape(g, 1, LANES), log_dt.reshape(g, 1, 1),
        _dup(jnp.swapaxes(b_re, 1, 2)), _dup(jnp.swapaxes(b_im, 1, 2)), _dup(c_re), _dup(c_im),
        jnp.tile(d, (1, S5_CHUNK)).reshape(g, 1, S5_FLAT),
    ]

    def per_group(*shape):
        return pl.BlockSpec((None,) + shape, lambda i: (i,) + (0,) * len(shape))

    in_specs = [per_group(1, LANES), per_group(1, LANES), per_group(1, 1),
                per_group(S5_GROUP, LANES), per_group(S5_GROUP, LANES),
                per_group(S5_GROUP, LANES), per_group(S5_GROUP, LANES), per_group(1, S5_FLAT)]
    out_specs = [per_group(S5_FLAT, S5_FLAT), per_group(S5_FLAT, LANES), per_group(S5_FLAT, LANES),
                 per_group(n_levels, LANES), per_group(n_levels, LANES)]
    out_shape = [jax.ShapeDtypeStruct((g, S5_FLAT, S5_FLAT), BF16),
                 jax.ShapeDtypeStruct((g, S5_FLAT, LANES), BF16),
                 jax.ShapeDtypeStruct((g, S5_FLAT, LANES), BF16),
                 jax.ShapeDtypeStruct((g, n_levels, LANES), F32),
                 jax.ShapeDtypeStruct((g, n_levels, LANES), F32)]
    return pl.pallas_call(
        functools.partial(_s5_prep_kernel, n_levels=n_levels),
        grid=(g,),
        in_specs=in_specs,
        out_specs=out_specs,
        out_shape=out_shape,
        scratch_shapes=[pltpu.VMEM(((S5_CHUNK + 1) * S5_GROUP, LANES), F32)],
        compiler_params=_params(("parallel",)),
        name="s5_prep",
    )(*args)


S5_GPB = LANES // S5_GROUP


def _s5_main_kernel(us_ref, t_ref, wend_ref, et_ref, da_ref, db_ref, ys_ref, y_sc, *, nb, n_levels):
    r = us_ref.shape[1] // nb
    rows = lax.broadcasted_iota(jnp.int32, (r, LANES), 0)
    piece = lax.broadcasted_iota(jnp.int32, (r, LANES), 1) // S5_GROUP

    def swap_pieces(arrs):
        arrs = list(arrs)
        d = S5_GPB // 2
        while d:
            hi = (piece & d) != 0
            up, down = S5_GROUP * d, LANES - S5_GROUP * d
            for i in range(S5_GPB):
                if i & d:
                    continue
                a, b = arrs[i], arrs[i + d]
                arrs[i] = jnp.where(hi, pltpu.roll(b, up, 1), a)
                arrs[i + d] = jnp.where(hi, b, pltpu.roll(a, down, 1))
            d //= 2
        return arrs

    def per_batch(b, carry):
        row0 = pl.multiple_of(b * r, r)
        xs = [us_ref[s, pl.ds(row0, r), :].astype(F32) for s in range(S5_CHUNK)]
        folded = [swap_pieces(xs[S5_GPB * hf:S5_GPB * (hf + 1)]) for hf in range(2)]
        for gl in range(S5_GPB):
            u = jnp.concatenate([folded[0][gl], folded[1][gl]], axis=1).astype(BF16)
            y = jnp.dot(u, t_ref[gl], preferred_element_type=F32)
            x = jnp.dot(u, wend_ref[gl], preferred_element_type=F32)
            da, db = da_ref[gl], db_ref[gl]
            for m in range(n_levels):
                d = 1 << m
                sh = jnp.where(rows >= d, pltpu.roll(x, d, 0), 0.0)
                x = x + sh * da[m:m + 1] + pltpu.roll(sh, S5_STATE, 1) * db[m:m + 1]
            hp = jnp.where(rows >= 1, pltpu.roll(x, 1, 0), 0.0).astype(BF16)
            y_sc[gl] = y + lax.dot_general(hp, et_ref[gl], _NT, preferred_element_type=F32)
        for hf in range(2):
            steps = swap_pieces([y_sc[gl, :, hf * LANES:(hf + 1) * LANES] for gl in range(S5_GPB)])
            for tp in range(S5_GPB):
                ys_ref[S5_GPB * hf + tp, pl.ds(row0, r), :] = steps[tp].astype(ys_ref.dtype)
        return carry

    lax.fori_loop(0, nb, per_batch, 0)


def _s5_main(us, ops, nb, n_levels):
    _, rows, width = us.shape
    t, wend, et, da, db = ops

    def per_block(*shape):
        return pl.BlockSpec((S5_GPB,) + shape, lambda i: (i,) + (0,) * len(shape))

    io = pl.BlockSpec((S5_CHUNK, rows, LANES), lambda i: (0, 0, i))
    return pl.pallas_call(
        functools.partial(_s5_main_kernel, nb=nb, n_levels=n_levels),
        grid=(width // LANES,),
        in_specs=[io, per_block(S5_FLAT, S5_FLAT), per_block(S5_FLAT, LANES),
                  per_block(S5_FLAT, LANES), per_block(n_levels, LANES), per_block(n_levels, LANES)],
        out_specs=io,
        out_shape=jax.ShapeDtypeStruct(us.shape, BF16),
        scratch_shapes=[pltpu.VMEM((S5_GPB, rows // nb, S5_FLAT), F32)],
        compiler_params=_params(("parallel",)),
        name="s5_main",
    )(us, t, wend, et, da, db)


S5_POST_TM = 1024


def _s5_post_kernel(y_ref, w_ref, o_ref):
    z = jax.nn.gelu(y_ref[...].astype(F32), approximate=True)
    gate = jnp.dot(z.astype(BF16), w_ref[...], preferred_element_type=F32)
    o_ref[...] = (z * jax.nn.sigmoid(gate)).astype(o_ref.dtype)


def _s5_post(y, w_glu):
    n, w = y.shape
    tm = min(S5_POST_TM, n)
    row = pl.BlockSpec((tm, w), lambda i: (i, 0))
    return pl.pallas_call(
        _s5_post_kernel,
        grid=(n // tm,),
        in_specs=[row, _resident(w_glu.shape)],
        out_specs=row,
        out_shape=jax.ShapeDtypeStruct((n, w), BF16),
        compiler_params=_params(("parallel",)),
        name="s5_post",
    )(y, w_glu)


def _s5_mixer(us, bsz, seq, prep_args, w_glu):
    r = seq // S5_CHUNK
    n_levels = int(math.log2(r))
    assert 1 << n_levels == r
    ops = _s5_prep(*prep_args, n_levels=n_levels)
    ys = _s5_main(us, ops, bsz, n_levels)
    return _s5_post(ys.reshape(bsz * seq, TOK_WIDTH), w_glu).reshape(us.shape)


ATTN_T = 256
QK_SCALE = DIFF_HEAD_DIM ** -0.5 * math.log2(math.e)


ATTN_HB = 6
ATTN_SUM_ROWS = 16


def _diff_attn_kernel(qt_ref, k_ref, vt_ref, lq1_ref, lk1_ref, lq2_ref, lk2_ref, g_ref, o_ref,
                      *, lam_init):
    t = ATTN_T
    hw = 2 * DIFF_HEAD_DIM
    i = pl.program_id(2)
    qz = []
    for h in range(ATTN_HB):
        qt = qt_ref[h * hw:(h + 1) * hw, :].astype(F32)
        sub = lax.broadcasted_iota(jnp.int32, qt.shape, 0) // DIFF_HEAD_DIM
        qz.append(jnp.concatenate([jnp.where(sub == c, qt, 0.0) for c in range(2)],
                                  axis=1).astype(BF16))

    def step(j, carry, mask):
        kb = k_ref[pl.ds(pl.multiple_of(j * t, t), t), :]
        vtb = vt_ref[j]
        ss = [jnp.dot(kb[:, h * hw:(h + 1) * hw], qz[h], preferred_element_type=F32)
              for h in range(ATTN_HB)]
        ones = jnp.ones((ATTN_SUM_ROWS, t), BF16)
        out = []
        for h in range(ATTN_HB):
            m, acc = carry[h]
            s = ss[h] if mask is None else jnp.where(mask, ss[h], -jnp.inf)
            m_new = jnp.maximum(m, jnp.max(s, axis=0, keepdims=True))
            alpha = jnp.exp2(m - m_new)
            p = jnp.exp2(s - m_new).astype(BF16)
            v1 = jnp.concatenate([vtb[h * DIFF_V_DIM:(h + 1) * DIFF_V_DIM, :], ones], axis=0)
            pv = jnp.dot(v1, p, preferred_element_type=F32)
            out.append((m_new, alpha * acc + pv))
        return tuple(out)

    init = tuple((jnp.full((1, 2 * t), -1e30, F32),
                  jnp.zeros((DIFF_V_DIM + ATTN_SUM_ROWS, 2 * t), F32)) for _ in range(ATTN_HB))
    carry = lax.fori_loop(0, i, lambda j, c: step(j, c, None), init)
    key = lax.broadcasted_iota(jnp.int32, (t, 2 * t), 0)
    qry = lax.broadcasted_iota(jnp.int32, (t, 2 * t), 1) % t
    carry = step(i, carry, key <= qry)

    lam = (jnp.exp(jnp.sum(lq1_ref[...] * lk1_ref[...], axis=-1, keepdims=True))
           - jnp.exp(jnp.sum(lq2_ref[...] * lk2_ref[...], axis=-1, keepdims=True)) + lam_init)
    for h in range(ATTN_HB):
        _, acc = carry[h]
        r = acc[:DIFF_V_DIM] / acc[DIFF_V_DIM:DIFF_V_DIM + 1]
        o = r[:, :t] - lam * r[:, t:]
        o = o * lax.rsqrt(jnp.mean(o * o, axis=0, keepdims=True) + EPS) * g_ref[...] * (1.0 - lam_init)
        o_ref[:, h * DIFF_V_DIM:(h + 1) * DIFF_V_DIM] = o.T.astype(o_ref.dtype)


def _diff_attn(qt, k, vt, bsz, seq, lq1, lk1, lq2, lk2, subln, lam_init):
    t = ATTN_T
    n = k.shape[0]
    nq = seq // t
    hw = ATTN_HB * DIFF_V_DIM
    vec = _resident((1, DIFF_HEAD_DIM))
    return pl.pallas_call(
        functools.partial(_diff_attn_kernel, lam_init=lam_init),
        grid=(bsz, DIFF_HEADS // ATTN_HB, nq),
        in_specs=[
            pl.BlockSpec((None, hw, t), lambda b, h, i: (b * nq + i, h, 0)),
            pl.BlockSpec((seq, hw), lambda b, h, i: (b, h)),
            pl.BlockSpec((nq, hw, t), lambda b, h, i: (b, h, 0)),
            vec, vec, vec, vec, _resident((DIFF_V_DIM, 1)),
        ],
        out_specs=pl.BlockSpec((t, hw), lambda b, h, i: (b * nq + i, h)),
        out_shape=jax.ShapeDtypeStruct((n, TOK_WIDTH), BF16),
        compiler_params=_params(("parallel", "parallel", "arbitrary")),
        name="diff_attn",
    )(qt, k, vt, lq1.reshape(1, -1), lk1.reshape(1, -1), lq2.reshape(1, -1), lk2.reshape(1, -1),
      subln.reshape(-1, 1))


MIX_TM = 1024


MEM_SUM_ROWS = 16


def _mix_out_kernel(x_ref, tok_ref, mqt_ref, mk_ref, mvt_ref, w_ref, o_ref, *scratch):
    if scratch:
        (tok_sc,) = scratch
        for t in range(TOK_WIDTH // LANES):
            for s in range(S5_CHUNK):
                tok_sc[t, pl.ds(s, tok_ref.shape[1], stride=S5_CHUNK), :] = (
                    tok_ref[s, :, t * LANES:(t + 1) * LANES].astype(F32))
        tok = jnp.concatenate([tok_sc[t] for t in range(TOK_WIDTH // LANES)], axis=1).astype(BF16)
    else:
        tok = tok_ref[...]
    acc = x_ref[...] + jnp.dot(tok, w_ref[:TOK_WIDTH, :], preferred_element_type=F32)

    mqt = jnp.concatenate([mqt_ref[j] for j in range(mqt_ref.shape[0])], axis=1)
    mqt = mqt.astype(F32) * MEM_HEAD_DIM ** -0.5
    head = lax.broadcasted_iota(jnp.int32, mqt.shape, 0) // MEM_HEAD_DIM
    mk = mk_ref[...]
    mvt = mvt_ref[...]
    ones = jnp.ones((MEM_SUM_ROWS, mvt.shape[1]), BF16)
    heads = []
    for h in range(MEM_HEADS):
        qz = jnp.where(head == h, mqt, 0.0).astype(BF16)
        s = jnp.dot(mk, qz, preferred_element_type=F32)
        p = jnp.exp(s - jnp.max(s, axis=0, keepdims=True)).astype(BF16)
        v1 = jnp.concatenate([mvt[h * MEM_HEAD_DIM:(h + 1) * MEM_HEAD_DIM, :], ones], axis=0)
        pv = jnp.dot(v1, p, preferred_element_type=F32)
        heads.append(pv[:MEM_HEAD_DIM] / pv[MEM_HEAD_DIM:MEM_HEAD_DIM + 1])
    mo = jnp.concatenate(heads, axis=0).T.astype(BF16)
    o_ref[...] = acc + jnp.dot(mo, w_ref[TOK_WIDTH:, :], preferred_element_type=F32)


def _mix_out(x, tok, mqt, mk, mvt, w, bsz, seq):
    n, d = x.shape
    tm = min(MIX_TM, seq)
    ns = seq // tm
    tq = mqt.shape[2]
    mq_block = mqt.shape[1] // MEM_WIDTH - 1
    row = pl.BlockSpec((tm, d), lambda b, i: (b * ns + i, 0))
    if tok.ndim == 3:
        tok_spec = pl.BlockSpec((S5_CHUNK, tm // S5_CHUNK, TOK_WIDTH), lambda b, i: (0, b * ns + i, 0))
        scratch = [pltpu.VMEM((TOK_WIDTH // LANES, tm, LANES), F32)]
    else:
        tok_spec = pl.BlockSpec((tm, TOK_WIDTH), lambda b, i: (b * ns + i, 0))
        scratch = []
    return pl.pallas_call(
        _mix_out_kernel,
        grid=(bsz, ns),
        in_specs=[
            row,
            tok_spec,
            pl.BlockSpec((tm // tq, MEM_WIDTH, tq), lambda b, i: (b * ns + i, mq_block, 0)),
            pl.BlockSpec((N_MEM, MEM_WIDTH), lambda b, i: (b, 0)),
            pl.BlockSpec((None, MEM_WIDTH, N_MEM), lambda b, i: (b, 0, 0)),
            _resident(w.shape),
        ],
        out_specs=row,
        out_shape=jax.ShapeDtypeStruct((n, d), F32),
        scratch_shapes=scratch,
        compiler_params=_params(("parallel", "parallel")),
        name="mix_out",
    )(x, tok, mqt, mk, mvt, w)


def kernel(x, mem, positions, ln_ffn1, ffn1_in, ffn1_out, ln_mix, w_mix_in, w_mix_out, ln_mem, w_mem_kv, ln_ffn2, ffn2_in, ffn2_out, s5_a_re, s5_a_im, s5_log_dt, s5_b_re, s5_b_im, s5_c_re, s5_c_im, s5_d, s5_w_glu, ln_kv, w_kv_shared, diff_lq1, diff_lk1, diff_lq2, diff_lk2, diff_subln, ln_final):
    bsz, seq, d = x.shape
    n = bsz * seq
    xf = x.reshape(n, d)
    memf = mem.reshape(bsz * N_MEM, d)
    rope = _rope_tables(positions)
    k_sh = vt_sh = None
    for i in range(DEPTH):
        if i == N_A:
            k_sh, vt_sh = _norm_proj(xf, ln_kv, w_kv_shared.astype(BF16), rope, QK_WIDTH,
                                     nat_cols=(0, QK_WIDTH), t_cols=(QK_WIDTH, 2 * QK_WIDTH),
                                     t_tile=ATTN_T)
        xf = _ffn(xf, ln_ffn1[i], ffn1_in[i], ffn1_out[i])
        if i < N_A:
            mqt, us = _norm_proj(xf, ln_mix[i], w_mix_in[i].astype(BF16),
                                 t_cols=(TOK_WIDTH, D_MODEL), sm_cols=(0, TOK_WIDTH))
            tok = _s5_mixer(us, bsz, seq,
                            (s5_a_re[i], s5_a_im[i], s5_log_dt[i], s5_b_re[i], s5_b_im[i],
                             s5_c_re[i], s5_c_im[i], s5_d[i]), s5_w_glu[i].astype(BF16))
        else:
            j = i - N_A
            mqt = _norm_proj(xf, ln_mix[i], w_mix_in[i].astype(BF16), rope, TOK_WIDTH, QK_SCALE,
                             t_cols=(0, D_MODEL), t_tile=ATTN_T)
            lam_init = 0.8 - 0.6 * math.exp(-0.3 * i)
            tok = _diff_attn(mqt, k_sh, vt_sh, bsz, seq, diff_lq1[j], diff_lk1[j], diff_lq2[j],
                             diff_lk2[j], diff_subln[j], lam_init)
        mk, mvt = _norm_proj(memf, ln_mem, w_mem_kv[i].astype(BF16),
                             nat_cols=(0, MEM_WIDTH), t_cols=(MEM_WIDTH, 2 * MEM_WIDTH), tm=N_MEM)
        xf = _mix_out(xf, tok, mqt, mk, mvt, w_mix_out[i].astype(BF16), bsz, seq)
        xf = _ffn(xf, ln_ffn2[i], ffn2_in[i], ffn2_out[i],
                  final_g=ln_final if i == DEPTH - 1 else None)
    return xf.reshape(bsz, seq, d)
```

```python
import functools
import math

import jax
import jax.numpy as jnp
from jax import lax
from jax.experimental import pallas as pl
from jax.experimental.pallas import tpu as pltpu

F32 = jnp.float32
BF16 = jnp.bfloat16

D_MODEL = 1024
DEPTH = 4
N_A = DEPTH // 2
TOK_WIDTH = 768
MEM_HEADS = 4
MEM_HEAD_DIM = 64
MEM_WIDTH = MEM_HEADS * MEM_HEAD_DIM
N_MEM = 256
S5_GROUP = 16
S5_GROUPS = TOK_WIDTH // S5_GROUP
S5_STATE = 64
S5_CHUNK = 16
S5_FLAT = S5_CHUNK * S5_GROUP
DIFF_HEAD_DIM = 64
DIFF_HEADS = 6
DIFF_V_DIM = 128
QK_WIDTH = 768
ROT_DIM = 16
ROPE_THETA = 500000.0
D_FF = 2816
EPS = 1e-6

LANES = 128
VMEM_LIMIT = 56 * 1024 * 1024

_NT = (((1,), (1,)), ((), ()))


def _params(sem, vmem=VMEM_LIMIT):
    return pltpu.CompilerParams(dimension_semantics=sem, vmem_limit_bytes=vmem)


def _resident(shape):
    nd = len(shape)
    return pl.BlockSpec(shape, lambda *_: (0,) * nd, pipeline_mode=pl.Buffered(1))


def _rms(x, g):
    return x * lax.rsqrt(jnp.mean(x * x, axis=-1, keepdims=True) + EPS) * g


FFN_TM = 512
FFN_TF = 256


def _ffn_kernel(x_ref, g_ref, win_ref, wout_ref, *rest, final):
    if final:
        gf_ref, o_ref = rest
    else:
        (o_ref,) = rest
    x = x_ref[...]
    h = _rms(x, g_ref[...]).astype(BF16)
    acc = x
    for c in range(D_FF // FFN_TF):
        lo = c * FFN_TF
        gate = jnp.dot(h, win_ref[:, lo:lo + FFN_TF].astype(BF16), preferred_element_type=F32)
        up = jnp.dot(h, win_ref[:, D_FF + lo:D_FF + lo + FFN_TF].astype(BF16),
                     preferred_element_type=F32)
        a = (gate * jax.nn.sigmoid(gate) * (0.5 * up)).astype(BF16)
        acc = acc + jnp.dot(a, wout_ref[lo:lo + FFN_TF, :].astype(BF16), preferred_element_type=F32)
    if final:
        acc = _rms(acc, gf_ref[...])
    o_ref[...] = acc


def _layer_resident(stack, layer):
    shape = stack.shape[1:]
    return pl.BlockSpec((None,) + shape, lambda *_: (layer,) + (0,) * len(shape),
                        pipeline_mode=pl.Buffered(1))


def _ffn(x, g, w_in, w_out, layer, final_g=None):
    n, d = x.shape
    row = pl.BlockSpec((FFN_TM, d), lambda i: (i, 0))
    in_specs = [row, _resident((1, d)), _layer_resident(w_in, layer), _layer_resident(w_out, layer)]
    args = [x, g.reshape(1, d), w_in, w_out]
    if final_g is not None:
        in_specs.append(_resident((1, d)))
        args.append(final_g.reshape(1, d))
    return pl.pallas_call(
        functools.partial(_ffn_kernel, final=final_g is not None),
        grid=(n // FFN_TM,),
        in_specs=in_specs,
        out_specs=row,
        out_shape=jax.ShapeDtypeStruct((n, d), F32),
        compiler_params=_params(("parallel",)),
        name="ffn",
    )(*args)


PROJ_TM = 512


def _rope_table_kernel(pos_ref, inv_ref, c_ref, sa_ref, sb_ref):
    ang = pos_ref[...].astype(F32) * inv_ref[...]
    lane = lax.broadcasted_iota(jnp.int32, ang.shape, 1) % DIFF_HEAD_DIM
    c = jnp.cos(ang)
    s = jnp.sin(ang)
    half = ROT_DIM // 2
    c_ref[...] = jnp.where(lane < ROT_DIM, c, 1.0)
    sa_ref[...] = jnp.where(lane < half, -s, 0.0)
    sb_ref[...] = jnp.where((lane >= half) & (lane < ROT_DIM), s, 0.0)


def _rope_tables(positions):
    n = positions.size
    half = ROT_DIM // 2
    inv = ROPE_THETA ** (-jnp.arange(0, ROT_DIM, 2, dtype=F32) / ROT_DIM)
    inv_lane = jnp.tile(inv, LANES // half).reshape(1, LANES)
    tm = 1024
    out = pl.BlockSpec((tm, LANES), lambda i: (i, 0))
    return pl.pallas_call(
        _rope_table_kernel,
        grid=(n // tm,),
        in_specs=[pl.BlockSpec((tm, 1), lambda i: (i, 0)), _resident((1, LANES))],
        out_specs=[out, out, out],
        out_shape=[jax.ShapeDtypeStruct((n, LANES), F32)] * 3,
        compiler_params=_params(("parallel",)),
        name="rope_tables",
    )(positions.reshape(n, 1), inv_lane)


def _norm_proj_kernel(x_ref, g_ref, w_ref, *rest, rope_cols, rope_scale, nat_cols, t_cols, sm_cols):
    h = _rms(x_ref[...], g_ref[...]).astype(BF16)
    y = jnp.dot(h, w_ref[...], preferred_element_type=F32)
    blocks = [y[:, t * LANES:(t + 1) * LANES] for t in range(y.shape[1] // LANES)]
    rest = list(rest)
    if rope_cols:
        c, sa, sb = (r[...] for r in rest[:3])
        rest = rest[3:]
        half = ROT_DIM // 2
        for t in range(rope_cols // LANES):
            blk = blocks[t]
            r = blk * c + pltpu.roll(blk, LANES - half, 1) * sa + pltpu.roll(blk, half, 1) * sb
            blocks[t] = r * rope_scale if rope_scale != 1.0 else r
    if nat_cols:
        o_ref = rest.pop(0)
        lo, hi = nat_cols
        for t in range(lo // LANES, hi // LANES):
            o_ref[:, t * LANES - lo:(t + 1) * LANES - lo] = blocks[t].astype(o_ref.dtype)
    if t_cols:
        ot_ref = rest.pop(0)
        lo, hi = t_cols
        for t in range(lo // LANES, hi // LANES):
            for j in range(ot_ref.shape[0]):
                tw = ot_ref.shape[2]
                ot_ref[j, t * LANES - lo:(t + 1) * LANES - lo, :] = (
                    blocks[t][j * tw:(j + 1) * tw, :].T.astype(ot_ref.dtype))
    if sm_cols:
        os_ref, sm_sc = rest
        lo, hi = sm_cols
        for t in range(lo // LANES, hi // LANES):
            sm_sc[t - lo // LANES] = blocks[t]
            for s in range(S5_CHUNK):
                os_ref[s, :, t * LANES - lo:(t + 1) * LANES - lo] = sm_sc[
                    t - lo // LANES, pl.ds(s, os_ref.shape[1], stride=S5_CHUNK), :].astype(os_ref.dtype)


def _norm_proj(x, g, w, rope=None, rope_cols=0, rope_scale=1.0, nat_cols=None, t_cols=None,
               sm_cols=None, tm=PROJ_TM, t_tile=None):
    n, d = x.shape
    nout = w.shape[1]
    tm = min(tm, n)
    if nat_cols is None and t_cols is None and sm_cols is None:
        nat_cols = (0, nout)
    in_specs = [pl.BlockSpec((tm, d), lambda i: (i, 0)), _resident((1, d)), _resident(w.shape)]
    args = [x, g.reshape(1, d), w]
    if rope_cols:
        tab = pl.BlockSpec((tm, LANES), lambda i: (i, 0))
        in_specs += [tab, tab, tab]
        args += list(rope)
    out_specs, out_shape = [], []
    if nat_cols:
        width = nat_cols[1] - nat_cols[0]
        out_specs.append(pl.BlockSpec((tm, width), lambda i: (i, 0)))
        out_shape.append(jax.ShapeDtypeStruct((n, width), BF16))
    if t_cols:
        width = t_cols[1] - t_cols[0]
        tw = min(t_tile or tm, tm)
        out_specs.append(pl.BlockSpec((tm // tw, width, tw), lambda i: (i, 0, 0)))
        out_shape.append(jax.ShapeDtypeStruct((n // tw, width, tw), BF16))
    scratch = []
    if sm_cols:
        width = sm_cols[1] - sm_cols[0]
        out_specs.append(pl.BlockSpec((S5_CHUNK, tm // S5_CHUNK, width), lambda i: (0, i, 0)))
        out_shape.append(jax.ShapeDtypeStruct((S5_CHUNK, n // S5_CHUNK, width), BF16))
        scratch.append(pltpu.VMEM((width // LANES, tm, LANES), F32))
    out = pl.pallas_call(
        functools.partial(_norm_proj_kernel, rope_cols=rope_cols, rope_scale=rope_scale,
                          nat_cols=nat_cols, t_cols=t_cols, sm_cols=sm_cols),
        grid=(n // tm,),
        in_specs=in_specs,
        out_specs=out_specs,
        out_shape=out_shape,
        scratch_shapes=scratch,
        compiler_params=_params(("parallel",)),
        name="norm_proj",
    )(*args)
    return out[0] if len(out) == 1 else out


S5_POW_ROWS = 24


def _s5_prep_kernel(are_ref, aim_ref, ldt_ref, btr_ref, bti_ref, cr_ref, ci_ref, dd_ref,
                    t_ref, wend_ref, et_ref, da_ref, db_ref, e_sc, *, n_levels):
    lr = are_ref[...]
    li = aim_ref[...]
    dt = jnp.exp(ldt_ref[...])
    lane = lax.broadcasted_iota(jnp.int32, (1, LANES), 1)
    lo = lane < S5_STATE

    def powers(k):
        mag = jnp.exp(k * (lr * dt))
        ang = k * (li * dt)
        return mag * jnp.cos(ang), mag * jnp.sin(ang)

    kf = lax.broadcasted_iota(jnp.int32, (S5_POW_ROWS, 1), 0).astype(F32)
    pr, pi = powers(kf)
    p_ri = jnp.where(lo, pr, pi)
    p_mir = jnp.where(lo, -pi, pr)
    p_rmi = jnp.where(lo, pr, -pi)
    p_mimr = jnp.where(lo, -pi, -pr)

    abr, abi = pr[1:2], pi[1:2]
    den = lr * lr + li * li
    nr, ni = abr - 1.0, abi
    fr = (nr * lr + ni * li) / den
    fi = (ni * lr - nr * li) / den
    btr, bti = btr_ref[...], bti_ref[...]
    bbr = fr * btr - fi * bti
    bbi = fr * bti + fi * btr

    cr, ci = cr_ref[...], ci_ref[...]
    for k in range(S5_CHUNK + 1):
        e_sc[k * S5_GROUP:(k + 1) * S5_GROUP, :] = cr * p_rmi[k:k + 1] + ci * p_mimr[k:k + 1]
    et_ref[...] = e_sc[S5_GROUP:, :].astype(et_ref.dtype)

    for s in range(S5_CHUNK):
        k = S5_CHUNK - 1 - s
        w = bbr * p_ri[k:k + 1] + bbi * p_mir[k:k + 1]
        wend_ref[s * S5_GROUP:(s + 1) * S5_GROUP, :] = w.astype(wend_ref.dtype)

    bb = jnp.where(lo, bbr, bbi)
    kern = lax.dot_general(bb, e_sc[:S5_FLAT, :], _NT, precision=lax.Precision.HIGHEST,
                           preferred_element_type=F32)
    col = lax.broadcasted_iota(jnp.int32, (S5_GROUP, S5_FLAT), 1)
    row = lax.broadcasted_iota(jnp.int32, (S5_GROUP, S5_FLAT), 0)
    dd = dd_ref[...]
    for s in range(S5_CHUNK):
        shifted = kern if s == 0 else pltpu.roll(kern, s * S5_GROUP, 1)
        blk = jnp.where(col >= s * S5_GROUP, shifted, 0.0)
        blk = blk + jnp.where(col == row + s * S5_GROUP, dd, 0.0)
        t_ref[s * S5_GROUP:(s + 1) * S5_GROUP, :] = blk.astype(t_ref.dtype)

    steps = (S5_CHUNK * 2.0 ** lax.broadcasted_iota(jnp.int32, (n_levels, 1), 0).astype(F32))
    qr, qi = powers(steps)
    da_ref[...] = qr
    db_ref[...] = jnp.where(lo, -qi, qi)


def _dup(a):
    return jnp.concatenate([a, a], axis=-1)


def _s5_prep(a_re, a_im, log_dt, b_re, b_im, c_re, c_im, d, n_levels):
    g = S5_GROUPS
    args = [
        _dup(a_re).reshape(g, 1, LANES), _dup(a_im).reshape(g, 1, LANES), log_dt.reshape(g, 1, 1),
        _dup(jnp.swapaxes(b_re, 1, 2)), _dup(jnp.swapaxes(b_im, 1, 2)), _dup(c_re), _dup(c_im),
        jnp.tile(d, (1, S5_CHUNK)).reshape(g, 1, S5_FLAT),
    ]

    def per_group(*shape):
        return pl.BlockSpec((None,) + shape, lambda i: (i,) + (0,) * len(shape))

    in_specs = [per_group(1, LANES), per_group(1, LANES), per_group(1, 1),
                per_group(S5_GROUP, LANES), per_group(S5_GROUP, LANES),
                per_group(S5_GROUP, LANES), per_group(S5_GROUP, LANES), per_group(1, S5_FLAT)]
    out_specs = [per_group(S5_FLAT, S5_FLAT), per_group(S5_FLAT, LANES), per_group(S5_FLAT, LANES),
                 per_group(n_levels, LANES), per_group(n_levels, LANES)]
    out_shape = [jax.ShapeDtypeStruct((g, S5_FLAT, S5_FLAT), BF16),
                 jax.ShapeDtypeStruct((g, S5_FLAT, LANES), BF16),
                 jax.ShapeDtypeStruct((g, S5_FLAT, LANES), BF16),
                 jax.ShapeDtypeStruct((g, n_levels, LANES), F32),
                 jax.ShapeDtypeStruct((g, n_levels, LANES), F32)]
    return pl.pallas_call(
        functools.partial(_s5_prep_kernel, n_levels=n_levels),
        grid=(g,),
        in_specs=in_specs,
        out_specs=out_specs,
        out_shape=out_shape,
        scratch_shapes=[pltpu.VMEM(((S5_CHUNK + 1) * S5_GROUP, LANES), F32)],
        compiler_params=_params(("parallel",)),
        name="s5_prep",
    )(*args)


S5_GPB = LANES // S5_GROUP


def _s5_main_kernel(us_ref, t_ref, wend_ref, et_ref, da_ref, db_ref, ys_ref, y_sc, *, nb, n_levels):
    r = us_ref.shape[1] // nb
    rows = lax.broadcasted_iota(jnp.int32, (r, LANES), 0)
    piece = lax.broadcasted_iota(jnp.int32, (r, LANES), 1) // S5_GROUP

    def swap_pieces(arrs):
        arrs = list(arrs)
        d = S5_GPB // 2
        while d:
            hi = (piece & d) != 0
            up, down = S5_GROUP * d, LANES - S5_GROUP * d
            for i in range(S5_GPB):
                if i & d:
                    continue
                a, b = arrs[i], arrs[i + d]
                arrs[i] = jnp.where(hi, pltpu.roll(b, up, 1), a)
                arrs[i + d] = jnp.where(hi, b, pltpu.roll(a, down, 1))
            d //= 2
        return arrs

    def per_batch(b, carry):
        row0 = pl.multiple_of(b * r, r)
        xs = [us_ref[s, pl.ds(row0, r), :].astype(F32) for s in range(S5_CHUNK)]
        folded = [swap_pieces(xs[S5_GPB * hf:S5_GPB * (hf + 1)]) for hf in range(2)]
        for gl in range(S5_GPB):
            u = jnp.concatenate([folded[0][gl], folded[1][gl]], axis=1).astype(BF16)
            y = jnp.dot(u, t_ref[gl], preferred_element_type=F32)
            x = jnp.dot(u, wend_ref[gl], preferred_element_type=F32)
            da, db = da_ref[gl], db_ref[gl]
            for m in range(n_levels):
                d = 1 << m
                sh = jnp.where(rows >= d, pltpu.roll(x, d, 0), 0.0)
                x = x + sh * da[m:m + 1] + pltpu.roll(sh, S5_STATE, 1) * db[m:m + 1]
            hp = jnp.where(rows >= 1, pltpu.roll(x, 1, 0), 0.0).astype(BF16)
            y_sc[gl] = y + lax.dot_general(hp, et_ref[gl], _NT, preferred_element_type=F32)
        for hf in range(2):
            steps = swap_pieces([y_sc[gl, :, hf * LANES:(hf + 1) * LANES] for gl in range(S5_GPB)])
            for tp in range(S5_GPB):
                ys_ref[S5_GPB * hf + tp, pl.ds(row0, r), :] = steps[tp].astype(ys_ref.dtype)
        return carry

    lax.fori_loop(0, nb, per_batch, 0)


def _s5_main(us, ops, nb, n_levels):
    _, rows, width = us.shape
    t, wend, et, da, db = ops

    def per_block(*shape):
        return pl.BlockSpec((S5_GPB,) + shape, lambda i: (i,) + (0,) * len(shape))

    io = pl.BlockSpec((S5_CHUNK, rows, LANES), lambda i: (0, 0, i))
    return pl.pallas_call(
        functools.partial(_s5_main_kernel, nb=nb, n_levels=n_levels),
        grid=(width // LANES,),
        in_specs=[io, per_block(S5_FLAT, S5_FLAT), per_block(S5_FLAT, LANES),
                  per_block(S5_FLAT, LANES), per_block(n_levels, LANES), per_block(n_levels, LANES)],
        out_specs=io,
        out_shape=jax.ShapeDtypeStruct(us.shape, BF16),
        scratch_shapes=[pltpu.VMEM((S5_GPB, rows // nb, S5_FLAT), F32)],
        compiler_params=_params(("parallel",)),
        name="s5_main",
    )(us, t, wend, et, da, db)


S5_POST_TM = 1024


def _s5_post_kernel(y_ref, w_ref, o_ref):
    z = jax.nn.gelu(y_ref[...].astype(F32), approximate=True)
    gate = jnp.dot(z.astype(BF16), w_ref[...], preferred_element_type=F32)
    o_ref[...] = (z * jax.nn.sigmoid(gate)).astype(o_ref.dtype)


def _s5_post(y, w_glu):
    n, w = y.shape
    tm = min(S5_POST_TM, n)
    row = pl.BlockSpec((tm, w), lambda i: (i, 0))
    return pl.pallas_call(
        _s5_post_kernel,
        grid=(n // tm,),
        in_specs=[row, _resident(w_glu.shape)],
        out_specs=row,
        out_shape=jax.ShapeDtypeStruct((n, w), BF16),
        compiler_params=_params(("parallel",)),
        name="s5_post",
    )(y, w_glu)


def _s5_mixer(us, bsz, seq, prep_args, w_glu):
    r = seq // S5_CHUNK
    n_levels = int(math.log2(r))
    assert 1 << n_levels == r
    ops = _s5_prep(*prep_args, n_levels=n_levels)
    ys = _s5_main(us, ops, bsz, n_levels)
    return _s5_post(ys.reshape(bsz * seq, TOK_WIDTH), w_glu).reshape(us.shape)


ATTN_T = 256
QK_SCALE = DIFF_HEAD_DIM ** -0.5 * math.log2(math.e)


ATTN_HB = 6
ATTN_SUM_ROWS = 16


def _diff_attn_kernel(qt_ref, k_ref, vt_ref, lq1_ref, lk1_ref, lq2_ref, lk2_ref, g_ref, o_ref,
                      *, lam_init):
    t = ATTN_T
    hw = 2 * DIFF_HEAD_DIM
    i = pl.program_id(2)
    qz = []
    for h in range(ATTN_HB):
        qt = qt_ref[h * hw:(h + 1) * hw, :].astype(F32)
        sub = lax.broadcasted_iota(jnp.int32, qt.shape, 0) // DIFF_HEAD_DIM
        qz.append(jnp.concatenate([jnp.where(sub == c, qt, 0.0) for c in range(2)],
                                  axis=1).astype(BF16))

    def step(j, carry, mask):
        kb = k_ref[pl.ds(pl.multiple_of(j * t, t), t), :]
        vtb = vt_ref[j]
        ss = [jnp.dot(kb[:, h * hw:(h + 1) * hw], qz[h], preferred_element_type=F32)
              for h in range(ATTN_HB)]
        ones = jnp.ones((ATTN_SUM_ROWS, t), BF16)
        out = []
        for h in range(ATTN_HB):
            m, acc = carry[h]
            s = ss[h] if mask is None else jnp.where(mask, ss[h], -jnp.inf)
            m_new = jnp.maximum(m, jnp.max(s, axis=0, keepdims=True))
            alpha = jnp.exp2(m - m_new)
            p = jnp.exp2(s - m_new).astype(BF16)
            v1 = jnp.concatenate([vtb[h * DIFF_V_DIM:(h + 1) * DIFF_V_DIM, :], ones], axis=0)
            pv = jnp.dot(v1, p, preferred_element_type=F32)
            out.append((m_new, alpha * acc + pv))
        return tuple(out)

    init = tuple((jnp.full((1, 2 * t), -1e30, F32),
                  jnp.zeros((DIFF_V_DIM + ATTN_SUM_ROWS, 2 * t), F32)) for _ in range(ATTN_HB))
    carry = lax.fori_loop(0, i, lambda j, c: step(j, c, None), init)
    key = lax.broadcasted_iota(jnp.int32, (t, 2 * t), 0)
    qry = lax.broadcasted_iota(jnp.int32, (t, 2 * t), 1) % t
    carry = step(i, carry, key <= qry)

    lam = (jnp.exp(jnp.sum(lq1_ref[...] * lk1_ref[...], axis=-1, keepdims=True))
           - jnp.exp(jnp.sum(lq2_ref[...] * lk2_ref[...], axis=-1, keepdims=True)) + lam_init)
    for h in range(ATTN_HB):
        _, acc = carry[h]
        r = acc[:DIFF_V_DIM] / acc[DIFF_V_DIM:DIFF_V_DIM + 1]
        o = r[:, :t] - lam * r[:, t:]
        o = o * lax.rsqrt(jnp.mean(o * o, axis=0, keepdims=True) + EPS) * g_ref[...] * (1.0 - lam_init)
        o_ref[:, h * DIFF_V_DIM:(h + 1) * DIFF_V_DIM] = o.T.astype(o_ref.dtype)


def _diff_attn(qt, k, vt, bsz, seq, lq1, lk1, lq2, lk2, subln, lam_init):
    t = ATTN_T
    n = k.shape[0]
    nq = seq // t
    hw = ATTN_HB * DIFF_V_DIM
    vec = _resident((1, DIFF_HEAD_DIM))
    return pl.pallas_call(
        functools.partial(_diff_attn_kernel, lam_init=lam_init),
        grid=(bsz, DIFF_HEADS // ATTN_HB, nq),
        in_specs=[
            pl.BlockSpec((None, hw, t), lambda b, h, i: (b * nq + i, h, 0)),
            pl.BlockSpec((seq, hw), lambda b, h, i: (b, h)),
            pl.BlockSpec((nq, hw, t), lambda b, h, i: (b, h, 0)),
            vec, vec, vec, vec, _resident((DIFF_V_DIM, 1)),
        ],
        out_specs=pl.BlockSpec((t, hw), lambda b, h, i: (b * nq + i, h)),
        out_shape=jax.ShapeDtypeStruct((n, TOK_WIDTH), BF16),
        compiler_params=_params(("parallel", "parallel", "arbitrary")),
        name="diff_attn",
    )(qt, k, vt, lq1.reshape(1, -1), lk1.reshape(1, -1), lq2.reshape(1, -1), lk2.reshape(1, -1),
      subln.reshape(-1, 1))


MIX_TM = 1024


MEM_SUM_ROWS = 16


def _mix_out_kernel(x_ref, tok_ref, mqt_ref, mk_ref, mvt_ref, w_ref, o_ref, *scratch):
    if scratch:
        (tok_sc,) = scratch
        for t in range(TOK_WIDTH // LANES):
            for s in range(S5_CHUNK):
                tok_sc[t, pl.ds(s, tok_ref.shape[1], stride=S5_CHUNK), :] = (
                    tok_ref[s, :, t * LANES:(t + 1) * LANES].astype(F32))
        tok = jnp.concatenate([tok_sc[t] for t in range(TOK_WIDTH // LANES)], axis=1).astype(BF16)
    else:
        tok = tok_ref[...]
    acc = x_ref[...] + jnp.dot(tok, w_ref[:TOK_WIDTH, :], preferred_element_type=F32)

    mqt = jnp.concatenate([mqt_ref[j] for j in range(mqt_ref.shape[0])], axis=1)
    mqt = mqt.astype(F32) * MEM_HEAD_DIM ** -0.5
    head = lax.broadcasted_iota(jnp.int32, mqt.shape, 0) // MEM_HEAD_DIM
    mk = mk_ref[...]
    mvt = mvt_ref[...]
    ones = jnp.ones((MEM_SUM_ROWS, mvt.shape[1]), BF16)
    heads = []
    for h in range(MEM_HEADS):
        qz = jnp.where(head == h, mqt, 0.0).astype(BF16)
        s = jnp.dot(mk, qz, preferred_element_type=F32)
        p = jnp.exp(s - jnp.max(s, axis=0, keepdims=True)).astype(BF16)
        v1 = jnp.concatenate([mvt[h * MEM_HEAD_DIM:(h + 1) * MEM_HEAD_DIM, :], ones], axis=0)
        pv = jnp.dot(v1, p, preferred_element_type=F32)
        heads.append(pv[:MEM_HEAD_DIM] / pv[MEM_HEAD_DIM:MEM_HEAD_DIM + 1])
    mo = jnp.concatenate(heads, axis=0).T.astype(BF16)
    o_ref[...] = acc + jnp.dot(mo, w_ref[TOK_WIDTH:, :], preferred_element_type=F32)


def _mix_out(x, tok, mqt, mk, mvt, w, bsz, seq):
    n, d = x.shape
    tm = min(MIX_TM, seq)
    ns = seq // tm
    tq = mqt.shape[2]
    mq_block = mqt.shape[1] // MEM_WIDTH - 1
    row = pl.BlockSpec((tm, d), lambda b, i: (b * ns + i, 0))
    if tok.ndim == 3:
        tok_spec = pl.BlockSpec((S5_CHUNK, tm // S5_CHUNK, TOK_WIDTH), lambda b, i: (0, b * ns + i, 0))
        scratch = [pltpu.VMEM((TOK_WIDTH // LANES, tm, LANES), F32)]
    else:
        tok_spec = pl.BlockSpec((tm, TOK_WIDTH), lambda b, i: (b * ns + i, 0))
        scratch = []
    return pl.pallas_call(
        _mix_out_kernel,
        grid=(bsz, ns),
        in_specs=[
            row,
            tok_spec,
            pl.BlockSpec((tm // tq, MEM_WIDTH, tq), lambda b, i: (b * ns + i, mq_block, 0)),
            pl.BlockSpec((N_MEM, MEM_WIDTH), lambda b, i: (b, 0)),
            pl.BlockSpec((None, MEM_WIDTH, N_MEM), lambda b, i: (b, 0, 0)),
            _resident(w.shape),
        ],
        out_specs=row,
        out_shape=jax.ShapeDtypeStruct((n, d), F32),
        scratch_shapes=scratch,
        compiler_params=_params(("parallel", "parallel")),
        name="mix_out",
    )(x, tok, mqt, mk, mvt, w)


def kernel(x, mem, positions, ln_ffn1, ffn1_in, ffn1_out, ln_mix, w_mix_in, w_mix_out, ln_mem, w_mem_kv, ln_ffn2, ffn2_in, ffn2_out, s5_a_re, s5_a_im, s5_log_dt, s5_b_re, s5_b_im, s5_c_re, s5_c_im, s5_d, s5_w_glu, ln_kv, w_kv_shared, diff_lq1, diff_lk1, diff_lq2, diff_lk2, diff_subln, ln_final):
    bsz, seq, d = x.shape
    n = bsz * seq
    xf = x.reshape(n, d)
    memf = mem.reshape(bsz * N_MEM, d)
    rope = _rope_tables(positions)
    k_sh = vt_sh = None
    for i in range(DEPTH):
        if i == N_A:
            k_sh, vt_sh = _norm_proj(xf, ln_kv, w_kv_shared.astype(BF16), rope, QK_WIDTH,
                                     nat_cols=(0, QK_WIDTH), t_cols=(QK_WIDTH, 2 * QK_WIDTH),
                                     t_tile=ATTN_T)
        xf = _ffn(xf, ln_ffn1[i], ffn1_in, ffn1_out, i)
        if i < N_A:
            mqt, us = _norm_proj(xf, ln_mix[i], w_mix_in[i].astype(BF16),
                                 t_cols=(TOK_WIDTH, D_MODEL), sm_cols=(0, TOK_WIDTH))
            tok = _s5_mixer(us, bsz, seq,
                            (s5_a_re[i], s5_a_im[i], s5_log_dt[i], s5_b_re[i], s5_b_im[i],
                             s5_c_re[i], s5_c_im[i], s5_d[i]), s5_w_glu[i].astype(BF16))
        else:
            j = i - N_A
            mqt = _norm_proj(xf, ln_mix[i], w_mix_in[i].astype(BF16), rope, TOK_WIDTH, QK_SCALE,
                             t_cols=(0, D_MODEL), t_tile=ATTN_T)
            lam_init = 0.8 - 0.6 * math.exp(-0.3 * i)
            tok = _diff_attn(mqt, k_sh, vt_sh, bsz, seq, diff_lq1[j], diff_lk1[j], diff_lq2[j],
                             diff_lk2[j], diff_subln[j], lam_init)
        mk, mvt = _norm_proj(memf, ln_mem, w_mem_kv[i].astype(BF16),
                             nat_cols=(0, MEM_WIDTH), t_cols=(MEM_WIDTH, 2 * MEM_WIDTH), tm=N_MEM)
        xf = _mix_out(xf, tok, mqt, mk, mvt, w_mix_out[i].astype(BF16), bsz, seq)
        xf = _ffn(xf, ln_ffn2[i], ffn2_in, ffn2_out, i,
                  final_g=ln_final if i == DEPTH - 1 else None)
    return xf.reshape(bsz, seq, d)
```

```python
import functools
import math

import jax
import jax.numpy as jnp
from jax import lax
from jax.experimental import pallas as pl
from jax.experimental.pallas import tpu as pltpu

F32 = jnp.float32
BF16 = jnp.bfloat16

D_MODEL = 1024
DEPTH = 4
N_A = DEPTH // 2
TOK_WIDTH = 768
MEM_HEADS = 4
MEM_HEAD_DIM = 64
MEM_WIDTH = MEM_HEADS * MEM_HEAD_DIM
N_MEM = 256
S5_GROUP = 16
S5_GROUPS = TOK_WIDTH // S5_GROUP
S5_STATE = 64
S5_CHUNK = 16
S5_FLAT = S5_CHUNK * S5_GROUP
DIFF_HEAD_DIM = 64
DIFF_HEADS = 6
DIFF_V_DIM = 128
QK_WIDTH = 768
ROT_DIM = 16
ROPE_THETA = 500000.0
D_FF = 2816
EPS = 1e-6

LANES = 128
VMEM_LIMIT = 56 * 1024 * 1024

_NT = (((1,), (1,)), ((), ()))


def _params(sem, vmem=VMEM_LIMIT):
    return pltpu.CompilerParams(dimension_semantics=sem, vmem_limit_bytes=vmem)


def _resident(shape):
    nd = len(shape)
    return pl.BlockSpec(shape, lambda *_: (0,) * nd, pipeline_mode=pl.Buffered(1))


def _rms(x, g):
    return x * lax.rsqrt(jnp.mean(x * x, axis=-1, keepdims=True) + EPS) * g


FFN_TM = 512
FFN_TF = 256


def _ffn_kernel(x_ref, g_ref, win_ref, wout_ref, *rest, final):
    if final:
        gf_ref, o_ref = rest
    else:
        (o_ref,) = rest
    x = x_ref[...]
    h = _rms(x, g_ref[...]).astype(BF16)
    acc = x
    for c in range(D_FF // FFN_TF):
        lo = c * FFN_TF
        gate = jnp.dot(h, win_ref[:, lo:lo + FFN_TF].astype(BF16), preferred_element_type=F32)
        up = jnp.dot(h, win_ref[:, D_FF + lo:D_FF + lo + FFN_TF].astype(BF16),
                     preferred_element_type=F32)
        a = (gate * jax.nn.sigmoid(gate) * (0.5 * up)).astype(BF16)
        acc = acc + jnp.dot(a, wout_ref[lo:lo + FFN_TF, :].astype(BF16), preferred_element_type=F32)
    if final:
        acc = _rms(acc, gf_ref[...])
    o_ref[...] = acc


def _layer_resident(stack, layer):
    shape = stack.shape[1:]
    return pl.BlockSpec((None,) + shape, lambda *_: (layer,) + (0,) * len(shape),
                        pipeline_mode=pl.Buffered(1))


def _ffn(x, g, w_in, w_out, layer, final_g=None):
    n, d = x.shape
    row = pl.BlockSpec((FFN_TM, d), lambda i: (i, 0))
    in_specs = [row, _resident((1, d)), _layer_resident(w_in, layer), _layer_resident(w_out, layer)]
    args = [x, g.reshape(1, d), w_in, w_out]
    if final_g is not None:
        in_specs.append(_resident((1, d)))
        args.append(final_g.reshape(1, d))
    return pl.pallas_call(
        functools.partial(_ffn_kernel, final=final_g is not None),
        grid=(n // FFN_TM,),
        in_specs=in_specs,
        out_specs=row,
        out_shape=jax.ShapeDtypeStruct((n, d), F32),
        compiler_params=_params(("parallel",)),
        name="ffn",
    )(*args)


PROJ_TM = 512


def _rope_table_kernel(pos_ref, inv_ref, c_ref, sa_ref, sb_ref):
    ang = pos_ref[...].astype(F32) * inv_ref[...]
    lane = lax.broadcasted_iota(jnp.int32, ang.shape, 1) % DIFF_HEAD_DIM
    c = jnp.cos(ang)
    s = jnp.sin(ang)
    half = ROT_DIM // 2
    c_ref[...] = jnp.where(lane < ROT_DIM, c, 1.0)
    sa_ref[...] = jnp.where(lane < half, -s, 0.0)
    sb_ref[...] = jnp.where((lane >= half) & (lane < ROT_DIM), s, 0.0)


def _rope_tables(positions):
    n = positions.size
    half = ROT_DIM // 2
    inv = ROPE_THETA ** (-jnp.arange(0, ROT_DIM, 2, dtype=F32) / ROT_DIM)
    inv_lane = jnp.tile(inv, LANES // half).reshape(1, LANES)
    tm = 1024
    out = pl.BlockSpec((tm, LANES), lambda i: (i, 0))
    return pl.pallas_call(
        _rope_table_kernel,
        grid=(n // tm,),
        in_specs=[pl.BlockSpec((tm, 1), lambda i: (i, 0)), _resident((1, LANES))],
        out_specs=[out, out, out],
        out_shape=[jax.ShapeDtypeStruct((n, LANES), F32)] * 3,
        compiler_params=_params(("parallel",)),
        name="rope_tables",
    )(positions.reshape(n, 1), inv_lane)


def _norm_proj_kernel(x_ref, g_ref, w_ref, *rest, rope_cols, rope_scale, nat_cols, t_cols, sm_cols):
    h = _rms(x_ref[...], g_ref[...]).astype(BF16)
    y = jnp.dot(h, w_ref[...], preferred_element_type=F32)
    blocks = [y[:, t * LANES:(t + 1) * LANES] for t in range(y.shape[1] // LANES)]
    rest = list(rest)
    if rope_cols:
        c, sa, sb = (r[...] for r in rest[:3])
        rest = rest[3:]
        half = ROT_DIM // 2
        for t in range(rope_cols // LANES):
            blk = blocks[t]
            r = blk * c + pltpu.roll(blk, LANES - half, 1) * sa + pltpu.roll(blk, half, 1) * sb
            blocks[t] = r * rope_scale if rope_scale != 1.0 else r
    if nat_cols:
        o_ref = rest.pop(0)
        lo, hi = nat_cols
        for t in range(lo // LANES, hi // LANES):
            o_ref[:, t * LANES - lo:(t + 1) * LANES - lo] = blocks[t].astype(o_ref.dtype)
    if t_cols:
        ot_ref = rest.pop(0)
        lo, hi = t_cols
        for t in range(lo // LANES, hi // LANES):
            for j in range(ot_ref.shape[0]):
                tw = ot_ref.shape[2]
                ot_ref[j, t * LANES - lo:(t + 1) * LANES - lo, :] = (
                    blocks[t][j * tw:(j + 1) * tw, :].T.astype(ot_ref.dtype))
    if sm_cols:
        os_ref, sm_sc = rest
        lo, hi = sm_cols
        for t in range(lo // LANES, hi // LANES):
            sm_sc[t - lo // LANES] = blocks[t]
            for s in range(S5_CHUNK):
                os_ref[s, :, t * LANES - lo:(t + 1) * LANES - lo] = sm_sc[
                    t - lo // LANES, pl.ds(s, os_ref.shape[1], stride=S5_CHUNK), :].astype(os_ref.dtype)


def _norm_proj(x, g, w, rope=None, rope_cols=0, rope_scale=1.0, nat_cols=None, t_cols=None,
               sm_cols=None, tm=PROJ_TM, t_tile=None):
    n, d = x.shape
    nout = w.shape[1]
    tm = min(tm, n)
    if nat_cols is None and t_cols is None and sm_cols is None:
        nat_cols = (0, nout)
    in_specs = [pl.BlockSpec((tm, d), lambda i: (i, 0)), _resident((1, d)), _resident(w.shape)]
    args = [x, g.reshape(1, d), w]
    if rope_cols:
        tab = pl.BlockSpec((tm, LANES), lambda i: (i, 0))
        in_specs += [tab, tab, tab]
        args += list(rope)
    out_specs, out_shape = [], []
    if nat_cols:
        width = nat_cols[1] - nat_cols[0]
        out_specs.append(pl.BlockSpec((tm, width), lambda i: (i, 0)))
        out_shape.append(jax.ShapeDtypeStruct((n, width), BF16))
    if t_cols:
        width = t_cols[1] - t_cols[0]
        tw = min(t_tile or tm, tm)
        out_specs.append(pl.BlockSpec((tm // tw, width, tw), lambda i: (i, 0, 0)))
        out_shape.append(jax.ShapeDtypeStruct((n // tw, width, tw), BF16))
    scratch = []
    if sm_cols:
        width = sm_cols[1] - sm_cols[0]
        out_specs.append(pl.BlockSpec((S5_CHUNK, tm // S5_CHUNK, width), lambda i: (0, i, 0)))
        out_shape.append(jax.ShapeDtypeStruct((S5_CHUNK, n // S5_CHUNK, width), BF16))
        scratch.append(pltpu.VMEM((width // LANES, tm, LANES), F32))
    out = pl.pallas_call(
        functools.partial(_norm_proj_kernel, rope_cols=rope_cols, rope_scale=rope_scale,
                          nat_cols=nat_cols, t_cols=t_cols, sm_cols=sm_cols),
        grid=(n // tm,),
        in_specs=in_specs,
        out_specs=out_specs,
        out_shape=out_shape,
        scratch_shapes=scratch,
        compiler_params=_params(("parallel",)),
        name="norm_proj",
    )(*args)
    return out[0] if len(out) == 1 else out


S5_POW_ROWS = 24


def _s5_prep_kernel(are_ref, aim_ref, ldt_ref, btr_ref, bti_ref, cr_ref, ci_ref, dd_ref,
                    t_ref, wend_ref, et_ref, da_ref, db_ref, e_sc, *, n_levels):
    lr = are_ref[...]
    li = aim_ref[...]
    dt = jnp.exp(ldt_ref[...])
    lane = lax.broadcasted_iota(jnp.int32, (1, LANES), 1)
    lo = lane < S5_STATE

    def powers(k):
        mag = jnp.exp(k * (lr * dt))
        ang = k * (li * dt)
        return mag * jnp.cos(ang), mag * jnp.sin(ang)

    kf = lax.broadcasted_iota(jnp.int32, (S5_POW_ROWS, 1), 0).astype(F32)
    pr, pi = powers(kf)
    p_ri = jnp.where(lo, pr, pi)
    p_mir = jnp.where(lo, -pi, pr)
    p_rmi = jnp.where(lo, pr, -pi)
    p_mimr = jnp.where(lo, -pi, -pr)

    abr, abi = pr[1:2], pi[1:2]
    den = lr * lr + li * li
    nr, ni = abr - 1.0, abi
    fr = (nr * lr + ni * li) / den
    fi = (ni * lr - nr * li) / den
    btr, bti = btr_ref[...], bti_ref[...]
    bbr = fr * btr - fi * bti
    bbi = fr * bti + fi * btr

    cr, ci = cr_ref[...], ci_ref[...]
    for k in range(S5_CHUNK + 1):
        e_sc[k * S5_GROUP:(k + 1) * S5_GROUP, :] = cr * p_rmi[k:k + 1] + ci * p_mimr[k:k + 1]
    et_ref[...] = e_sc[S5_GROUP:, :].astype(et_ref.dtype)

    for s in range(S5_CHUNK):
        k = S5_CHUNK - 1 - s
        w = bbr * p_ri[k:k + 1] + bbi * p_mir[k:k + 1]
        wend_ref[s * S5_GROUP:(s + 1) * S5_GROUP, :] = w.astype(wend_ref.dtype)

    bb = jnp.where(lo, bbr, bbi)
    kern = lax.dot_general(bb, e_sc[:S5_FLAT, :], _NT, precision=lax.Precision.HIGHEST,
                           preferred_element_type=F32)
    col = lax.broadcasted_iota(jnp.int32, (S5_GROUP, S5_FLAT), 1)
    row = lax.broadcasted_iota(jnp.int32, (S5_GROUP, S5_FLAT), 0)
    dd = dd_ref[...]
    for s in range(S5_CHUNK):
        shifted = kern if s == 0 else pltpu.roll(kern, s * S5_GROUP, 1)
        blk = jnp.where(col >= s * S5_GROUP, shifted, 0.0)
        blk = blk + jnp.where(col == row + s * S5_GROUP, dd, 0.0)
        t_ref[s * S5_GROUP:(s + 1) * S5_GROUP, :] = blk.astype(t_ref.dtype)

    steps = (S5_CHUNK * 2.0 ** lax.broadcasted_iota(jnp.int32, (n_levels, 1), 0).astype(F32))
    qr, qi = powers(steps)
    da_ref[...] = qr
    db_ref[...] = jnp.where(lo, -qi, qi)


def _dup(a):
    return jnp.concatenate([a, a], axis=-1)


def _s5_prep(a_re, a_im, log_dt, b_re, b_im, c_re, c_im, d, n_levels):
    g = S5_GROUPS
    args = [
        _dup(a_re).reshape(g, 1, LANES), _dup(a_im).reshape(g, 1, LANES), log_dt.reshape(g, 1, 1),
        _dup(jnp.swapaxes(b_re, 1, 2)), _dup(jnp.swapaxes(b_im, 1, 2)), _dup(c_re), _dup(c_im),
        jnp.tile(d, (1, S5_CHUNK)).reshape(g, 1, S5_FLAT),
    ]

    def per_group(*shape):
        return pl.BlockSpec((None,) + shape, lambda i: (i,) + (0,) * len(shape))

    in_specs = [per_group(1, LANES), per_group(1, LANES), per_group(1, 1),
                per_group(S5_GROUP, LANES), per_group(S5_GROUP, LANES),
                per_group(S5_GROUP, LANES), per_group(S5_GROUP, LANES), per_group(1, S5_FLAT)]
    out_specs = [per_group(S5_FLAT, S5_FLAT), per_group(S5_FLAT, LANES), per_group(S5_FLAT, LANES),
                 per_group(n_levels, LANES), per_group(n_levels, LANES)]
    out_shape = [jax.ShapeDtypeStruct((g, S5_FLAT, S5_FLAT), BF16),
                 jax.ShapeDtypeStruct((g, S5_FLAT, LANES), BF16),
                 jax.ShapeDtypeStruct((g, S5_FLAT, LANES), BF16),
                 jax.ShapeDtypeStruct((g, n_levels, LANES), F32),
                 jax.ShapeDtypeStruct((g, n_levels, LANES), F32)]
    return pl.pallas_call(
        functools.partial(_s5_prep_kernel, n_levels=n_levels),
        grid=(g,),
        in_specs=in_specs,
        out_specs=out_specs,
        out_shape=out_shape,
        scratch_shapes=[pltpu.VMEM(((S5_CHUNK + 1) * S5_GROUP, LANES), F32)],
        compiler_params=_params(("parallel",)),
        name="s5_prep",
    )(*args)


S5_GPB = LANES // S5_GROUP


def _s5_main_kernel(us_ref, t_ref, wend_ref, et_ref, da_ref, db_ref, ys_ref, y_sc, *, nb, n_levels):
    r = us_ref.shape[1] // nb
    rows = lax.broadcasted_iota(jnp.int32, (r, LANES), 0)
    piece = lax.broadcasted_iota(jnp.int32, (r, LANES), 1) // S5_GROUP

    def swap_pieces(arrs):
        arrs = list(arrs)
        d = S5_GPB // 2
        while d:
            hi = (piece & d) != 0
            up, down = S5_GROUP * d, LANES - S5_GROUP * d
            for i in range(S5_GPB):
                if i & d:
                    continue
                a, b = arrs[i], arrs[i + d]
                arrs[i] = jnp.where(hi, pltpu.roll(b, up, 1), a)
                arrs[i + d] = jnp.where(hi, b, pltpu.roll(a, down, 1))
            d //= 2
        return arrs

    def per_batch(b, carry):
        row0 = pl.multiple_of(b * r, r)
        xs = [us_ref[s, pl.ds(row0, r), :].astype(F32) for s in range(S5_CHUNK)]
        folded = [swap_pieces(xs[S5_GPB * hf:S5_GPB * (hf + 1)]) for hf in range(2)]
        for gl in range(S5_GPB):
            u = jnp.concatenate([folded[0][gl], folded[1][gl]], axis=1).astype(BF16)
            y = jnp.dot(u, t_ref[gl], preferred_element_type=F32)
            x = jnp.dot(u, wend_ref[gl], preferred_element_type=F32)
            da, db = da_ref[gl], db_ref[gl]
            for m in range(n_levels):
                d = 1 << m
                sh = jnp.where(rows >= d, pltpu.roll(x, d, 0), 0.0)
                x = x + sh * da[m:m + 1] + pltpu.roll(sh, S5_STATE, 1) * db[m:m + 1]
            hp = jnp.where(rows >= 1, pltpu.roll(x, 1, 0), 0.0).astype(BF16)
            y_sc[gl] = y + lax.dot_general(hp, et_ref[gl], _NT, preferred_element_type=F32)
        for hf in range(2):
            steps = swap_pieces([y_sc[gl, :, hf * LANES:(hf + 1) * LANES] for gl in range(S5_GPB)])
            for tp in range(S5_GPB):
                ys_ref[S5_GPB * hf + tp, pl.ds(row0, r), :] = steps[tp].astype(ys_ref.dtype)
        return carry

    lax.fori_loop(0, nb, per_batch, 0)


def _s5_main(us, ops, nb, n_levels):
    _, rows, width = us.shape
    t, wend, et, da, db = ops

    def per_block(*shape):
        return pl.BlockSpec((S5_GPB,) + shape, lambda i: (i,) + (0,) * len(shape))

    io = pl.BlockSpec((S5_CHUNK, rows, LANES), lambda i: (0, 0, i))
    return pl.pallas_call(
        functools.partial(_s5_main_kernel, nb=nb, n_levels=n_levels),
        grid=(width // LANES,),
        in_specs=[io, per_block(S5_FLAT, S5_FLAT), per_block(S5_FLAT, LANES),
                  per_block(S5_FLAT, LANES), per_block(n_levels, LANES), per_block(n_levels, LANES)],
        out_specs=io,
        out_shape=jax.ShapeDtypeStruct(us.shape, BF16),
        scratch_shapes=[pltpu.VMEM((S5_GPB, rows // nb, S5_FLAT), F32)],
        compiler_params=_params(("parallel",)),
        name="s5_main",
    )(us, t, wend, et, da, db)


def _s5_mixer(us, bsz, seq, prep_args):
    r = seq // S5_CHUNK
    n_levels = int(math.log2(r))
    assert 1 << n_levels == r
    ops = _s5_prep(*prep_args, n_levels=n_levels)
    return _s5_main(us, ops, bsz, n_levels)


ATTN_T = 256
QK_SCALE = DIFF_HEAD_DIM ** -0.5 * math.log2(math.e)


ATTN_HB = 6
ATTN_SUM_ROWS = 16


def _diff_attn_kernel(qt_ref, k_ref, vt_ref, lq1_ref, lk1_ref, lq2_ref, lk2_ref, g_ref, o_ref,
                      *, lam_init):
    t = ATTN_T
    hw = 2 * DIFF_HEAD_DIM
    i = pl.program_id(2)
    qz = []
    for h in range(ATTN_HB):
        qt = qt_ref[h * hw:(h + 1) * hw, :].astype(F32)
        sub = lax.broadcasted_iota(jnp.int32, qt.shape, 0) // DIFF_HEAD_DIM
        qz.append(jnp.concatenate([jnp.where(sub == c, qt, 0.0) for c in range(2)],
                                  axis=1).astype(BF16))

    def step(j, carry, mask):
        kb = k_ref[pl.ds(pl.multiple_of(j * t, t), t), :]
        vtb = vt_ref[j]
        ss = [jnp.dot(kb[:, h * hw:(h + 1) * hw], qz[h], preferred_element_type=F32)
              for h in range(ATTN_HB)]
        ones = jnp.ones((ATTN_SUM_ROWS, t), BF16)
        out = []
        for h in range(ATTN_HB):
            m, acc = carry[h]
            s = ss[h] if mask is None else jnp.where(mask, ss[h], -jnp.inf)
            m_new = jnp.maximum(m, jnp.max(s, axis=0, keepdims=True))
            alpha = jnp.exp2(m - m_new)
            p = jnp.exp2(s - m_new).astype(BF16)
            v1 = jnp.concatenate([vtb[h * DIFF_V_DIM:(h + 1) * DIFF_V_DIM, :], ones], axis=0)
            pv = jnp.dot(v1, p, preferred_element_type=F32)
            out.append((m_new, alpha * acc + pv))
        return tuple(out)

    init = tuple((jnp.full((1, 2 * t), -1e30, F32),
                  jnp.zeros((DIFF_V_DIM + ATTN_SUM_ROWS, 2 * t), F32)) for _ in range(ATTN_HB))
    carry = lax.fori_loop(0, i, lambda j, c: step(j, c, None), init)
    key = lax.broadcasted_iota(jnp.int32, (t, 2 * t), 0)
    qry = lax.broadcasted_iota(jnp.int32, (t, 2 * t), 1) % t
    carry = step(i, carry, key <= qry)

    lam = (jnp.exp(jnp.sum(lq1_ref[...] * lk1_ref[...], axis=-1, keepdims=True))
           - jnp.exp(jnp.sum(lq2_ref[...] * lk2_ref[...], axis=-1, keepdims=True)) + lam_init)
    for h in range(ATTN_HB):
        _, acc = carry[h]
        r = acc[:DIFF_V_DIM] / acc[DIFF_V_DIM:DIFF_V_DIM + 1]
        o = r[:, :t] - lam * r[:, t:]
        o = o * lax.rsqrt(jnp.mean(o * o, axis=0, keepdims=True) + EPS) * g_ref[...] * (1.0 - lam_init)
        o_ref[:, h * DIFF_V_DIM:(h + 1) * DIFF_V_DIM] = o.T.astype(o_ref.dtype)


def _diff_attn(qt, k, vt, bsz, seq, lq1, lk1, lq2, lk2, subln, lam_init):
    t = ATTN_T
    n = k.shape[0]
    nq = seq // t
    hw = ATTN_HB * DIFF_V_DIM
    vec = _resident((1, DIFF_HEAD_DIM))
    return pl.pallas_call(
        functools.partial(_diff_attn_kernel, lam_init=lam_init),
        grid=(bsz, DIFF_HEADS // ATTN_HB, nq),
        in_specs=[
            pl.BlockSpec((None, hw, t), lambda b, h, i: (b * nq + i, h, 0)),
            pl.BlockSpec((seq, hw), lambda b, h, i: (b, h)),
            pl.BlockSpec((nq, hw, t), lambda b, h, i: (b, h, 0)),
            vec, vec, vec, vec, _resident((DIFF_V_DIM, 1)),
        ],
        out_specs=pl.BlockSpec((t, hw), lambda b, h, i: (b * nq + i, h)),
        out_shape=jax.ShapeDtypeStruct((n, TOK_WIDTH), BF16),
        compiler_params=_params(("parallel", "parallel", "arbitrary")),
        name="diff_attn",
    )(qt, k, vt, lq1.reshape(1, -1), lk1.reshape(1, -1), lq2.reshape(1, -1), lk2.reshape(1, -1),
      subln.reshape(-1, 1))


MIX_TM = 1024


MEM_SUM_ROWS = 16


def _mix_out_kernel(x_ref, tok_ref, mqt_ref, mk_ref, mvt_ref, w_ref, *rest):
    if len(rest) == 3:
        wglu_ref, o_ref, tok_sc = rest
        for t in range(TOK_WIDTH // LANES):
            for s in range(S5_CHUNK):
                tok_sc[t, pl.ds(s, tok_ref.shape[1], stride=S5_CHUNK), :] = (
                    tok_ref[s, :, t * LANES:(t + 1) * LANES].astype(F32))
        y = jnp.concatenate([tok_sc[t] for t in range(TOK_WIDTH // LANES)], axis=1)
        z = jax.nn.gelu(y, approximate=True)
        gate = jnp.dot(z.astype(BF16), wglu_ref[...], preferred_element_type=F32)
        tok = (z * jax.nn.sigmoid(gate)).astype(BF16)
    else:
        (o_ref,) = rest
        tok = tok_ref[...]
    acc = x_ref[...] + jnp.dot(tok, w_ref[:TOK_WIDTH, :], preferred_element_type=F32)

    mqt = jnp.concatenate([mqt_ref[j] for j in range(mqt_ref.shape[0])], axis=1)
    mqt = mqt.astype(F32) * MEM_HEAD_DIM ** -0.5
    head = lax.broadcasted_iota(jnp.int32, mqt.shape, 0) // MEM_HEAD_DIM
    mk = mk_ref[...]
    mvt = mvt_ref[...]
    ones = jnp.ones((MEM_SUM_ROWS, mvt.shape[1]), BF16)
    heads = []
    for h in range(MEM_HEADS):
        qz = jnp.where(head == h, mqt, 0.0).astype(BF16)
        s = jnp.dot(mk, qz, preferred_element_type=F32)
        p = jnp.exp(s - jnp.max(s, axis=0, keepdims=True)).astype(BF16)
        v1 = jnp.concatenate([mvt[h * MEM_HEAD_DIM:(h + 1) * MEM_HEAD_DIM, :], ones], axis=0)
        pv = jnp.dot(v1, p, preferred_element_type=F32)
        heads.append(pv[:MEM_HEAD_DIM] / pv[MEM_HEAD_DIM:MEM_HEAD_DIM + 1])
    mo = jnp.concatenate(heads, axis=0).T.astype(BF16)
    o_ref[...] = acc + jnp.dot(mo, w_ref[TOK_WIDTH:, :], preferred_element_type=F32)


def _mix_out(x, tok, mqt, mk, mvt, w, bsz, seq, w_glu=None):
    n, d = x.shape
    tm = min(MIX_TM, seq)
    ns = seq // tm
    tq = mqt.shape[2]
    mq_block = mqt.shape[1] // MEM_WIDTH - 1
    row = pl.BlockSpec((tm, d), lambda b, i: (b * ns + i, 0))
    in_specs = [
        row,
        None,
        pl.BlockSpec((tm // tq, MEM_WIDTH, tq), lambda b, i: (b * ns + i, mq_block, 0)),
        pl.BlockSpec((N_MEM, MEM_WIDTH), lambda b, i: (b, 0)),
        pl.BlockSpec((None, MEM_WIDTH, N_MEM), lambda b, i: (b, 0, 0)),
        _resident(w.shape),
    ]
    args = [x, tok, mqt, mk, mvt, w]
    if w_glu is not None:
        in_specs[1] = pl.BlockSpec((S5_CHUNK, tm // S5_CHUNK, TOK_WIDTH),
                                   lambda b, i: (0, b * ns + i, 0))
        in_specs.append(_resident(w_glu.shape))
        args.append(w_glu)
        scratch = [pltpu.VMEM((TOK_WIDTH // LANES, tm, LANES), F32)]
    else:
        in_specs[1] = pl.BlockSpec((tm, TOK_WIDTH), lambda b, i: (b * ns + i, 0))
        scratch = []
    return pl.pallas_call(
        _mix_out_kernel,
        grid=(bsz, ns),
        in_specs=in_specs,
        out_specs=row,
        out_shape=jax.ShapeDtypeStruct((n, d), F32),
        scratch_shapes=scratch,
        compiler_params=_params(("parallel", "parallel")),
        name="mix_out",
    )(*args)


def kernel(x, mem, positions, ln_ffn1, ffn1_in, ffn1_out, ln_mix, w_mix_in, w_mix_out, ln_mem, w_mem_kv, ln_ffn2, ffn2_in, ffn2_out, s5_a_re, s5_a_im, s5_log_dt, s5_b_re, s5_b_im, s5_c_re, s5_c_im, s5_d, s5_w_glu, ln_kv, w_kv_shared, diff_lq1, diff_lk1, diff_lq2, diff_lk2, diff_subln, ln_final):
    bsz, seq, d = x.shape
    n = bsz * seq
    xf = x.reshape(n, d)
    memf = mem.reshape(bsz * N_MEM, d)
    rope = _rope_tables(positions)
    k_sh = vt_sh = None
    for i in range(DEPTH):
        if i == N_A:
            k_sh, vt_sh = _norm_proj(xf, ln_kv, w_kv_shared.astype(BF16), rope, QK_WIDTH,
                                     nat_cols=(0, QK_WIDTH), t_cols=(QK_WIDTH, 2 * QK_WIDTH),
                                     t_tile=ATTN_T)
        xf = _ffn(xf, ln_ffn1[i], ffn1_in, ffn1_out, i)
        if i < N_A:
            mqt, us = _norm_proj(xf, ln_mix[i], w_mix_in[i].astype(BF16),
                                 t_cols=(TOK_WIDTH, D_MODEL), sm_cols=(0, TOK_WIDTH), tm=MIX_TM)
            tok = _s5_mixer(us, bsz, seq,
                            (s5_a_re[i], s5_a_im[i], s5_log_dt[i], s5_b_re[i], s5_b_im[i],
                             s5_c_re[i], s5_c_im[i], s5_d[i]))
            w_glu = s5_w_glu[i].astype(BF16)
        else:
            w_glu = None
            j = i - N_A
            mqt = _norm_proj(xf, ln_mix[i], w_mix_in[i].astype(BF16), rope, TOK_WIDTH, QK_SCALE,
                             t_cols=(0, D_MODEL), t_tile=ATTN_T)
            lam_init = 0.8 - 0.6 * math.exp(-0.3 * i)
            tok = _diff_attn(mqt, k_sh, vt_sh, bsz, seq, diff_lq1[j], diff_lk1[j], diff_lq2[j],
                             diff_lk2[j], diff_subln[j], lam_init)
        mk, mvt = _norm_proj(memf, ln_mem, w_mem_kv[i].astype(BF16),
                             nat_cols=(0, MEM_WIDTH), t_cols=(MEM_WIDTH, 2 * MEM_WIDTH), tm=N_MEM)
        xf = _mix_out(xf, tok, mqt, mk, mvt, w_mix_out[i].astype(BF16), bsz, seq, w_glu)
        xf = _ffn(xf, ln_ffn2[i], ffn2_in, ffn2_out, i,
                  final_g=ln_final if i == DEPTH - 1 else None)
    return xf.reshape(bsz, seq, d)
```
